```python
import math
import jax, jax.numpy as jnp
from jax import lax
import numpy as np

D_MODEL = 1024
BATCH = 4
SEQ = 8192
DEPTH = 2

PLE_DIM = 256
RET_HEADS = 4
RET_DK = 128
RET_DV = 128
RET_CHUNK = 128
POOL_WINDOWS = (2, 4, 8, 16)
POOL_GROUP = 128
POOL_WIDTH = len(POOL_WINDOWS) * POOL_GROUP
AB_IN = RET_HEADS * (2 * RET_DK + 2 * RET_DV) + POOL_WIDTH
AB_MIX = RET_HEADS * RET_DV + POOL_WIDTH
C_HEADS = 16
C_HEAD_DIM = 64
C_Q_RANK = 256
C_KV_RANK = 256
IDX_HEADS = 8
IDX_DIM = 64
IDX_TOPK_MAX = 256
Q_BLOCK = 128
C_IN = C_Q_RANK + C_KV_RANK + IDX_HEADS * IDX_DIM + IDX_DIM + IDX_HEADS
N_GROUPS = 4
EXPERTS_PER_GROUP = 8
N_EXPERTS = N_GROUPS * EXPERTS_PER_GROUP
TOPK_IN_GROUP = 2
EXPERT_FF = 512
MOE_BLOCK = 128
ALPHA = (2.0 * DEPTH) ** 0.25
BETA = (8.0 * DEPTH) ** -0.25
LN_EPS = 1e-5
N_EVEN = (DEPTH + 1) // 2
N_ODD = DEPTH // 2

kernel_name = "hybrid_retention_pool_dsa_hmoe"


def layer_norm(x, g, b):
    xf = x.astype(jnp.float32)
    mu = xf.mean(-1, keepdims=True)
    var = jnp.square(xf - mu).mean(-1, keepdims=True)
    return ((xf - mu) * lax.rsqrt(var + LN_EPS) * g + b).astype(x.dtype)


def rms_norm(x, g):
    xf = x.astype(jnp.float32)
    return (xf * lax.rsqrt(jnp.mean(xf * xf, -1, keepdims=True) + 1e-6) * g).astype(x.dtype)


def retention_chunkwise(q, k, v):
    B, T, H, dk = q.shape
    dv = v.shape[-1]
    C = RET_CHUNK
    N = T // C
    log_g = jnp.log1p(-jnp.exp2(-5.0 - jnp.arange(H, dtype=jnp.float32)))
    i = jnp.arange(C, dtype=jnp.float32)
    diff = i[:, None] - i[None, :]
    inner_decay = jnp.where(diff >= 0, jnp.exp(log_g[:, None, None] * jnp.maximum(diff, 0.0)), 0.0)
    q_decay = jnp.exp(log_g[:, None] * (i + 1.0))
    k_decay = jnp.exp(log_g[:, None] * (C - 1.0 - i))
    chunk_decay = jnp.exp(log_g * C)

    def to_chunks(a):
        return a.astype(jnp.float32).reshape(B, N, C, H, a.shape[-1]).transpose(1, 0, 3, 2, 4)

    def step(R, inp):
        qi, ki, vi = inp
        s = jnp.einsum('bhqd,bhkd->bhqk', qi, ki) * inner_decay
        o = jnp.einsum('bhqk,bhkv->bhqv', s, vi) + jnp.einsum('bhqd,bhdv->bhqv', qi * q_decay[..., None], R)
        R = R * chunk_decay[:, None, None] + jnp.einsum('bhkd,bhkv->bhdv', ki * k_decay[..., None], vi)
        return R, o

    R0 = jnp.zeros((B, H, dk, dv), jnp.float32)
    _, o = lax.scan(step, R0, (to_chunks(q), to_chunks(k), to_chunks(v)))
    return o.transpose(1, 0, 3, 2, 4).reshape(B, T, H, dv)


def causal_multiscale_pool(u, w_group, scale):
    B, T, _ = u.shape
    nG = len(POOL_WINDOWS)
    uf = u.astype(jnp.float32).reshape(B, T, nG, POOL_GROUP)
    c = jnp.concatenate([jnp.zeros((B, 1, nG, POOL_GROUP), jnp.float32), jnp.cumsum(uf, axis=1)], axis=1)
    t = jnp.arange(T)
    means = []
    for gi, w in enumerate(POOL_WINDOWS):
        cg = c[:, :, gi]
        lo = jnp.maximum(t + 1 - w, 0)
        cnt = jnp.minimum(t + 1, w).astype(jnp.float32)
        means.append((cg[:, 1:] - cg[:, lo]) / cnt[None, :, None])
    d = jnp.stack(means, axis=2) - uf
    y = jnp.einsum('btgc,gcd->btgd', d, w_group.astype(jnp.float32)) * scale.reshape(nG, POOL_GROUP)
    return y.reshape(B, T, nG * POOL_GROUP)


def mixer_ab(x, w_in, gn_g, pool_w, pool_scale, w_out):
    B, T, _ = x.shape
    nq = RET_HEADS * RET_DK
    nv = RET_HEADS * RET_DV
    z = x @ w_in
    q, k, v, g, u = jnp.split(z, [nq, 2 * nq, 2 * nq + nv, 2 * nq + 2 * nv], axis=-1)
    q = q.reshape(B, T, RET_HEADS, RET_DK)
    k = k.reshape(B, T, RET_HEADS, RET_DK) * (RET_DK ** -0.5)
    v = v.reshape(B, T, RET_HEADS, RET_DV)
    o = retention_chunkwise(q, k, v)
    mu = o.mean(-1, keepdims=True)
    var = jnp.square(o - mu).mean(-1, keepdims=True)
    o = ((o - mu) * lax.rsqrt(var + LN_EPS)).reshape(B, T, nv) * gn_g
    ret = jax.nn.silu(g.astype(jnp.float32)) * o
    pool = causal_multiscale_pool(u, pool_w, pool_scale)
    mixed = jnp.concatenate([ret, pool], axis=-1).astype(x.dtype)
    return mixed @ w_out


def alibi_slopes(n):
    return jnp.exp2(-8.0 * jnp.arange(1, n + 1, dtype=jnp.float32) / n)


def mixer_c(x, w_in, q_norm, kv_norm, w_uq, w_uk, w_uv, w_out):
    B, T, _ = x.shape
    topk = min(IDX_TOPK_MAX, T // 4)
    z = x @ w_in
    s1 = C_Q_RANK
    s2 = s1 + C_KV_RANK
    s3 = s2 + IDX_HEADS * IDX_DIM
    s4 = s3 + IDX_DIM
    c_q, c_kv, q_idx, k_idx, w_idx = jnp.split(z, [s1, s2, s3, s4], axis=-1)
    c_q = rms_norm(c_q, q_norm)
    c_kv = rms_norm(c_kv, kv_norm)
    q_idx = q_idx.reshape(B, T, IDX_HEADS, IDX_DIM)
    slopes = alibi_slopes(C_HEADS)
    scale = C_HEAD_DIM ** -0.5
    pos = jnp.arange(T)

    def block(bi):
        start = bi * Q_BLOCK
        tq = start + jnp.arange(Q_BLOCK)
        qb = lax.dynamic_slice_in_dim(q_idx, start, Q_BLOCK, axis=1)
        wb = lax.dynamic_slice_in_dim(w_idx, start, Q_BLOCK, axis=1)
        cqb = lax.dynamic_slice_in_dim(c_q, start, Q_BLOCK, axis=1)
        sc = jax.nn.relu(jnp.einsum('bqhd,bsd->bqhs', qb, k_idx))
        I = jnp.einsum('bqh,bqhs->bqs', wb, sc).astype(jnp.float32)
        I = jnp.where((pos[None, :] <= tq[:, None])[None], I, -jnp.inf)
        _, idx = lax.top_k(I, topk)
        valid = idx <= tq[None, :, None]
        c_sel = jax.vmap(lambda c, i: c[i])(c_kv, idx)
        qh = (cqb @ w_uq).reshape(B, Q_BLOCK, C_HEADS, C_HEAD_DIM)
        q_abs = jnp.einsum('bqhd,chd->bqhc', qh, w_uk)
        logits = jnp.einsum('bqhc,bqkc->bqhk', q_abs, c_sel).astype(jnp.float32) * scale
        dist = (tq[None, :, None] - idx).astype(jnp.float32)
        logits = logits - slopes[None, None, :, None] * dist[:, :, None, :]
        logits = jnp.where(valid[:, :, None, :], logits, -jnp.inf)
        prob = jax.nn.softmax(logits, axis=-1).astype(c_sel.dtype)
        o_lat = jnp.einsum('bqhk,bqkc->bqhc', prob, c_sel)
        o = jnp.einsum('bqhc,chd->bqhd', o_lat, w_uv)
        return o.reshape(B, Q_BLOCK, C_HEADS * C_HEAD_DIM)

    o = lax.map(block, jnp.arange(T // Q_BLOCK))
    o = o.transpose(1, 0, 2, 3).reshape(B, T, C_HEADS * C_HEAD_DIM)
    return o @ w_out


def hierarchical_moe(x, w_group, b_group, w_expert, b_expert, w_gu, w_down):
    B, T, D = x.shape
    N = B * T
    xt = x.reshape(N, D)
    ar = jnp.arange(N)
    g_logits = (xt @ w_group).astype(jnp.float32) + b_group
    g_prob = jax.nn.softmax(g_logits, axis=-1)
    _, grp = lax.top_k(g_logits, 1)
    grp = grp[:, 0]
    g_w = g_prob[ar, grp]
    e_logits = ((xt @ w_expert).astype(jnp.float32) + b_expert).reshape(N, N_GROUPS, EXPERTS_PER_GROUP)
    e_logits = e_logits[ar, grp]
    top_v, top_i = lax.top_k(e_logits, TOPK_IN_GROUP)
    gate = g_w[:, None] * jax.nn.softmax(top_v, axis=-1)
    expert = grp[:, None] * EXPERTS_PER_GROUP + top_i
    M = N * TOPK_IN_GROUP
    flat_e = expert.reshape(M)
    flat_tok = jnp.repeat(ar, TOPK_IN_GROUP)
    flat_gate = gate.reshape(M)
    order = jnp.argsort(flat_e)
    e_sorted = flat_e[order]
    counts = jnp.bincount(flat_e, length=N_EXPERTS)
    start = jnp.cumsum(counts) - counts
    padded = ((counts + MOE_BLOCK - 1) // MOE_BLOCK) * MOE_BLOCK
    pend = jnp.cumsum(padded)
    pstart = pend - padded
    dest = pstart[e_sorted] + (jnp.arange(M) - start[e_sorted])
    n_blocks = -(-(M + N_EXPERTS * (MOE_BLOCK - 1)) // MOE_BLOCK)
    R = n_blocks * MOE_BLOCK
    row_tok = jnp.zeros((R,), jnp.int32).at[dest].set(flat_tok[order].astype(jnp.int32))
    row_gate = jnp.zeros((R,), jnp.float32).at[dest].set(flat_gate[order])
    blk_e = jnp.minimum(jnp.searchsorted(pend, jnp.arange(n_blocks) * MOE_BLOCK, side='right'), N_EXPERTS - 1)

    def run(args):
        tok, e = args
        h = xt[tok] @ w_gu[e]
        a, b = jnp.split(h, 2, axis=-1)
        return (jax.nn.silu(a) * b) @ w_down[e]

    ys = lax.map(run, (row_tok.reshape(n_blocks, MOE_BLOCK), blk_e)).reshape(R, D)
    y = jnp.zeros((N, D), jnp.float32).at[row_tok].add(ys.astype(jnp.float32) * row_gate[:, None])
    return y.astype(x.dtype).reshape(B, T, D)


def setup_inputs(seed: int = 0) -> dict:
    key = jax.random.key(seed)
    ks = iter(jax.random.split(key, 40))
    f32 = jnp.float32

    def nrm(shape, s):
        return jax.random.normal(next(ks), shape, f32) * s

    def gain(shape):
        return 1.0 + 0.02 * jax.random.normal(next(ks), shape, f32)

    nq = RET_HEADS * RET_DK
    nv = RET_HEADS * RET_DV
    ab_col_scale = jnp.ones((AB_IN,), f32).at[2 * nq:2 * nq + nv].set(BETA)
    return {
        "x": nrm((BATCH, SEQ, D_MODEL), 1.0),
        "p": nrm((DEPTH, BATCH, SEQ, PLE_DIM), 1.0),
        "ab_w_in": nrm((N_EVEN, D_MODEL, AB_IN), D_MODEL ** -0.5) * ab_col_scale,
        "ret_gn_g": gain((N_EVEN, nv)),
        "pool_w": nrm((N_EVEN, len(POOL_WINDOWS), POOL_GROUP, POOL_GROUP), POOL_GROUP ** -0.5 * BETA),
        "pool_scale": gain((N_EVEN, POOL_WIDTH)),
        "ab_w_out": nrm((N_EVEN, AB_MIX, D_MODEL), AB_MIX ** -0.5 * BETA),
        "c_w_in": nrm((N_ODD, D_MODEL, C_IN), D_MODEL ** -0.5),
        "c_q_norm": gain((N_ODD, C_Q_RANK)),
        "c_kv_norm": gain((N_ODD, C_KV_RANK)),
        "c_w_uq": nrm((N_ODD, C_Q_RANK, C_HEADS * C_HEAD_DIM), C_Q_RANK ** -0.5),
        "c_w_uk": nrm((N_ODD, C_KV_RANK, C_HEADS, C_HEAD_DIM), C_KV_RANK ** -0.5),
        "c_w_uv": nrm((N_ODD, C_KV_RANK, C_HEADS, C_HEAD_DIM), C_KV_RANK ** -0.5 * BETA),
        "c_w_out": nrm((N_ODD, C_HEADS * C_HEAD_DIM, D_MODEL), (C_HEADS * C_HEAD_DIM) ** -0.5 * BETA),
        "ln1_g": gain((DEPTH, D_MODEL)),
        "ln1_b": nrm((DEPTH, D_MODEL), 0.02),
        "ln2_g": gain((DEPTH, D_MODEL)),
        "ln2_b": nrm((DEPTH, D_MODEL), 0.02),
        "moe_w_group": nrm((DEPTH, D_MODEL, N_GROUPS), D_MODEL ** -0.5),
        "moe_b_group": nrm((DEPTH, N_GROUPS), 0.01),
        "moe_w_expert": nrm((DEPTH, D_MODEL, N_EXPERTS), D_MODEL ** -0.5),
        "moe_b_expert": nrm((DEPTH, N_EXPERTS), 0.01),
        "moe_w_gu": nrm((DEPTH, N_EXPERTS, D_MODEL, 2 * EXPERT_FF), D_MODEL ** -0.5),
        "moe_w_down": nrm((DEPTH, N_EXPERTS, EXPERT_FF, D_MODEL), EXPERT_FF ** -0.5 * BETA),
        "ple_w_gate": nrm((DEPTH, D_MODEL, D_MODEL), D_MODEL ** -0.5),
        "ple_w_proj": nrm((DEPTH, PLE_DIM, D_MODEL), PLE_DIM ** -0.5),
    }


def reference(x, p, ab_w_in, ret_gn_g, pool_w, pool_scale, ab_w_out,
              c_w_in, c_q_norm, c_kv_norm, c_w_uq, c_w_uk, c_w_uv, c_w_out,
              ln1_g, ln1_b, ln2_g, ln2_b,
              moe_w_group, moe_b_group, moe_w_expert, moe_b_expert, moe_w_gu, moe_w_down,
              ple_w_gate, ple_w_proj):
    for i in range(DEPTH):
        j = i // 2
        if i % 2 == 0:
            h = mixer_ab(x, ab_w_in[j], ret_gn_g[j], pool_w[j], pool_scale[j], ab_w_out[j])
        else:
            h = mixer_c(x, c_w_in[j], c_q_norm[j], c_kv_norm[j], c_w_uq[j], c_w_uk[j], c_w_uv[j], c_w_out[j])
        x = layer_norm(ALPHA * x + h.astype(x.dtype), ln1_g[i], ln1_b[i])
        m = hierarchical_moe(x, moe_w_group[i], moe_b_group[i], moe_w_expert[i], moe_b_expert[i],
                             moe_w_gu[i], moe_w_down[i])
        x = layer_norm(ALPHA * x + m, ln2_g[i], ln2_b[i])
        x = x + jax.nn.sigmoid(x @ ple_w_gate[i]) * (p[i] @ ple_w_proj[i])
    return x
```

```python
import functools
import math
import jax
import jax.numpy as jnp
from jax import lax
from jax.experimental import pallas as pl
from jax.experimental.pallas import tpu as pltpu

D_MODEL = 1024
DEPTH = 2
RET_HEADS = 4
RET_DK = 128
RET_DV = 128
RET_CHUNK = 128
POOL_WINDOWS = (2, 4, 8, 16)
POOL_GROUP = 128
C_HEADS = 16
C_HEAD_DIM = 64
C_Q_RANK = 256
C_KV_RANK = 256
IDX_HEADS = 8
IDX_DIM = 64
IDX_TOPK_MAX = 256
Q_BLOCK = 128
N_GROUPS = 4
EXPERTS_PER_GROUP = 8
N_EXPERTS = N_GROUPS * EXPERTS_PER_GROUP
TOPK_IN_GROUP = 2
MOE_BLOCK = 128
ALPHA = (2.0 * DEPTH) ** 0.25
LN_EPS = 1e-5


def _mm_kernel(x_ref, w_ref, o_ref):
    o_ref[...] = jnp.dot(x_ref[...].astype(jnp.bfloat16), w_ref[...].astype(jnp.bfloat16),
                         preferred_element_type=jnp.float32)


def _matmul(x, w, tm=512):
    M, K = x.shape
    N = w.shape[1]
    npad = (-N) % 128
    if npad:
        w = jnp.pad(w, ((0, 0), (0, npad)))
    Np = N + npad
    out = pl.pallas_call(
        _mm_kernel,
        grid=(M // tm,),
        in_specs=[pl.BlockSpec((tm, K), lambda i: (i, 0)),
                  pl.BlockSpec((K, Np), lambda i: (0, 0))],
        out_specs=pl.BlockSpec((tm, Np), lambda i: (i, 0)),
        out_shape=jax.ShapeDtypeStruct((M, Np), jnp.float32),
        compiler_params=pltpu.CompilerParams(dimension_semantics=("arbitrary",),
                                             vmem_limit_bytes=56 * 1024 * 1024),
    )(x, w)
    return out[:, :N] if npad else out


def _mm3(x, w):
    B, T, K = x.shape
    return _matmul(x.reshape(B * T, K), w).reshape(B, T, w.shape[1])


def _layer_norm(x, g, b):
    mu = x.mean(-1, keepdims=True)
    var = jnp.square(x - mu).mean(-1, keepdims=True)
    return (x - mu) * lax.rsqrt(var + LN_EPS) * g + b


def _rms_norm(x, g):
    return x * lax.rsqrt(jnp.mean(x * x, -1, keepdims=True) + 1e-6) * g


def _retention(q, k, v):
    B, T, H, dk = q.shape
    C = RET_CHUNK
    N = T // C
    log_g = jnp.log1p(-jnp.exp2(-5.0 - jnp.arange(H, dtype=jnp.float32)))
    i = jnp.arange(C, dtype=jnp.float32)
    diff = i[:, None] - i[None, :]
    inner_decay = jnp.where(diff >= 0, jnp.exp(log_g[:, None, None] * jnp.maximum(diff, 0.0)), 0.0)
    q_decay = jnp.exp(log_g[:, None] * (i + 1.0))
    k_decay = jnp.exp(log_g[:, None] * (C - 1.0 - i))
    chunk_decay = jnp.exp(log_g * C)

    def to_chunks(a):
        return a.reshape(B, N, C, H, a.shape[-1]).transpose(1, 0, 3, 2, 4)

    def step(R, inp):
        qi, ki, vi = inp
        s = jnp.einsum('bhqd,bhkd->bhqk', qi, ki) * inner_decay
        o = jnp.einsum('bhqk,bhkv->bhqv', s, vi) + jnp.einsum('bhqd,bhdv->bhqv', qi * q_decay[..., None], R)
        R = R * chunk_decay[:, None, None] + jnp.einsum('bhkd,bhkv->bhdv', ki * k_decay[..., None], vi)
        return R, o

    R0 = jnp.zeros((B, H, dk, v.shape[-1]), jnp.float32)
    _, o = lax.scan(step, R0, (to_chunks(q), to_chunks(k), to_chunks(v)))
    return o.transpose(1, 0, 3, 2, 4).reshape(B, T, H, v.shape[-1])


def _pool(u, w_group, scale):
    B, T, _ = u.shape
    nG = len(POOL_WINDOWS)
    uf = u.reshape(B, T, nG, POOL_GROUP)
    c = jnp.concatenate([jnp.zeros((B, 1, nG, POOL_GROUP), jnp.float32), jnp.cumsum(uf, axis=1)], axis=1)
    t = jnp.arange(T)
    means = []
    for gi, w in enumerate(POOL_WINDOWS):
        cg = c[:, :, gi]
        lo = jnp.maximum(t + 1 - w, 0)
        cnt = jnp.minimum(t + 1, w).astype(jnp.float32)
        means.append((cg[:, 1:] - cg[:, lo]) / cnt[None, :, None])
    d = jnp.stack(means, axis=2) - uf
    y = jnp.einsum('btgc,gcd->btgd', d, w_group) * scale.reshape(nG, POOL_GROUP)
    return y.reshape(B, T, nG * POOL_GROUP)


def _mixer_ab(x, w_in, gn_g, pool_w, pool_scale, w_out):
    B, T, _ = x.shape
    nq = RET_HEADS * RET_DK
    nv = RET_HEADS * RET_DV
    z = _mm3(x, w_in)
    q, k, v, g, u = jnp.split(z, [nq, 2 * nq, 2 * nq + nv, 2 * nq + 2 * nv], axis=-1)
    q = q.reshape(B, T, RET_HEADS, RET_DK)
    k = k.reshape(B, T, RET_HEADS, RET_DK) * (RET_DK ** -0.5)
    v = v.reshape(B, T, RET_HEADS, RET_DV)
    o = _retention(q, k, v)
    mu = o.mean(-1, keepdims=True)
    var = jnp.square(o - mu).mean(-1, keepdims=True)
    o = ((o - mu) * lax.rsqrt(var + LN_EPS)).reshape(B, T, nv) * gn_g
    ret = jax.nn.silu(g) * o
    pool = _pool(u, pool_w, pool_scale)
    mixed = jnp.concatenate([ret, pool], axis=-1)
    return _mm3(mixed, w_out)


def _mixer_c(x, w_in, q_norm, kv_norm, w_uq, w_uk, w_uv, w_out):
    B, T, _ = x.shape
    topk = min(IDX_TOPK_MAX, T // 4)
    z = _mm3(x, w_in)
    s1 = C_Q_RANK
    s2 = s1 + C_KV_RANK
    s3 = s2 + IDX_HEADS * IDX_DIM
    s4 = s3 + IDX_DIM
    c_q, c_kv, q_idx, k_idx, w_idx = jnp.split(z, [s1, s2, s3, s4], axis=-1)
    c_q = _rms_norm(c_q, q_norm)
    c_kv = _rms_norm(c_kv, kv_norm)
    q_idx = q_idx.reshape(B, T, IDX_HEADS, IDX_DIM)
    slopes = jnp.exp2(-8.0 * jnp.arange(1, C_HEADS + 1, dtype=jnp.float32) / C_HEADS)
    scale = C_HEAD_DIM ** -0.5
    pos = jnp.arange(T)

    def block(bi):
        start = bi * Q_BLOCK
        tq = start + jnp.arange(Q_BLOCK)
        qb = lax.dynamic_slice_in_dim(q_idx, start, Q_BLOCK, axis=1)
        wb = lax.dynamic_slice_in_dim(w_idx, start, Q_BLOCK, axis=1)
        cqb = lax.dynamic_slice_in_dim(c_q, start, Q_BLOCK, axis=1)
        sc = jax.nn.relu(jnp.einsum('bqhd,bsd->bqhs', qb, k_idx))
        I = jnp.einsum('bqh,bqhs->bqs', wb, sc)
        I = jnp.where((pos[None, :] <= tq[:, None])[None], I, -jnp.inf)
        _, idx = lax.top_k(I, topk)
        valid = idx <= tq[None, :, None]
        c_sel = jax.vmap(lambda c, i: c[i])(c_kv, idx)
        qh = (cqb @ w_uq).reshape(B, Q_BLOCK, C_HEADS, C_HEAD_DIM)
        q_abs = jnp.einsum('bqhd,chd->bqhc', qh, w_uk)
        logits = jnp.einsum('bqhc,bqkc->bqhk', q_abs, c_sel) * scale
        dist = (tq[None, :, None] - idx).astype(jnp.float32)
        logits = logits - slopes[None, None, :, None] * dist[:, :, None, :]
        logits = jnp.where(valid[:, :, None, :], logits, -jnp.inf)
        prob = jax.nn.softmax(logits, axis=-1)
        o_lat = jnp.einsum('bqhk,bqkc->bqhc', prob, c_sel)
        o = jnp.einsum('bqhc,chd->bqhd', o_lat, w_uv)
        return o.reshape(B, Q_BLOCK, C_HEADS * C_HEAD_DIM)

    o = lax.map(block, jnp.arange(T // Q_BLOCK))
    o = o.transpose(1, 0, 2, 3).reshape(B, T, C_HEADS * C_HEAD_DIM)
    return _mm3(o, w_out)


def _moe(x, w_group, b_group, w_expert, b_expert, w_gu, w_down):
    B, T, D = x.shape
    N = B * T
    xt = x.reshape(N, D)
    ar = jnp.arange(N)
    g_logits = xt @ w_group + b_group
    g_prob = jax.nn.softmax(g_logits, axis=-1)
    _, grp = lax.top_k(g_logits, 1)
    grp = grp[:, 0]
    g_w = g_prob[ar, grp]
    e_logits = (xt @ w_expert + b_expert).reshape(N, N_GROUPS, EXPERTS_PER_GROUP)
    e_logits = e_logits[ar, grp]
    top_v, top_i = lax.top_k(e_logits, TOPK_IN_GROUP)
    gate = g_w[:, None] * jax.nn.softmax(top_v, axis=-1)
    expert = grp[:, None] * EXPERTS_PER_GROUP + top_i
    M = N * TOPK_IN_GROUP
    flat_e = expert.reshape(M)
    flat_tok = jnp.repeat(ar, TOPK_IN_GROUP)
    flat_gate = gate.reshape(M)
    order = jnp.argsort(flat_e)
    e_sorted = flat_e[order]
    counts = jnp.bincount(flat_e, length=N_EXPERTS)
    start = jnp.cumsum(counts) - counts
    padded = ((counts + MOE_BLOCK - 1) // MOE_BLOCK) * MOE_BLOCK
    pend = jnp.cumsum(padded)
    pstart = pend - padded
    dest = pstart[e_sorted] + (jnp.arange(M) - start[e_sorted])
    n_blocks = -(-(M + N_EXPERTS * (MOE_BLOCK - 1)) // MOE_BLOCK)
    R = n_blocks * MOE_BLOCK
    row_tok = jnp.zeros((R,), jnp.int32).at[dest].set(flat_tok[order].astype(jnp.int32))
    row_gate = jnp.zeros((R,), jnp.float32).at[dest].set(flat_gate[order])
    blk_e = jnp.minimum(jnp.searchsorted(pend, jnp.arange(n_blocks) * MOE_BLOCK, side='right'), N_EXPERTS - 1)

    def run(args):
        tok, e = args
        h = xt[tok] @ w_gu[e]
        a, b = jnp.split(h, 2, axis=-1)
        return (jax.nn.silu(a) * b) @ w_down[e]

    ys = lax.map(run, (row_tok.reshape(n_blocks, MOE_BLOCK), blk_e)).reshape(R, D)
    y = jnp.zeros((N, D), jnp.float32).at[row_tok].add(ys * row_gate[:, None])
    return y.reshape(B, T, D)


def kernel(x, p, ab_w_in, ret_gn_g, pool_w, pool_scale, ab_w_out, c_w_in, c_q_norm, c_kv_norm, c_w_uq, c_w_uk, c_w_uv, c_w_out, ln1_g, ln1_b, ln2_g, ln2_b, moe_w_group, moe_b_group, moe_w_expert, moe_b_expert, moe_w_gu, moe_w_down, ple_w_gate, ple_w_proj):
    for i in range(DEPTH):
        j = i // 2
        if i % 2 == 0:
            h = _mixer_ab(x, ab_w_in[j], ret_gn_g[j], pool_w[j], pool_scale[j], ab_w_out[j])
        else:
            h = _mixer_c(x, c_w_in[j], c_q_norm[j], c_kv_norm[j], c_w_uq[j], c_w_uk[j], c_w_uv[j], c_w_out[j])
        x = _layer_norm(ALPHA * x + h, ln1_g[i], ln1_b[i])
        m = _moe(x, moe_w_group[i], moe_b_group[i], moe_w_expert[i], moe_b_expert[i],
                 moe_w_gu[i], moe_w_down[i])
        x = _layer_norm(ALPHA * x + m, ln2_g[i], ln2_b[i])
        x = x + jax.nn.sigmoid(_mm3(x, ple_w_gate[i])) * _mm3(p[i], ple_w_proj[i])
    return x
```

```python
import functools
import math
import jax
import jax.numpy as jnp
from jax import lax
from jax.experimental import pallas as pl
from jax.experimental.pallas import tpu as pltpu

D_MODEL = 1024
DEPTH = 2
RET_HEADS = 4
RET_DK = 128
RET_DV = 128
RET_CHUNK = 128
POOL_WINDOWS = (2, 4, 8, 16)
POOL_GROUP = 128
C_HEADS = 16
C_HEAD_DIM = 64
C_Q_RANK = 256
C_KV_RANK = 256
IDX_HEADS = 8
IDX_DIM = 64
IDX_TOPK_MAX = 256
Q_BLOCK = 128
N_GROUPS = 4
EXPERTS_PER_GROUP = 8
N_EXPERTS = N_GROUPS * EXPERTS_PER_GROUP
TOPK_IN_GROUP = 2
MOE_BLOCK = 128
ALPHA = (2.0 * DEPTH) ** 0.25
LN_EPS = 1e-5


def _mm_kernel(x_ref, w_ref, o_ref):
    o_ref[...] = jnp.dot(x_ref[...].astype(jnp.bfloat16), w_ref[...].astype(jnp.bfloat16),
                         preferred_element_type=jnp.float32)


def _matmul(x, w, tm=512):
    M, K = x.shape
    N = w.shape[1]
    npad = (-N) % 128
    if npad:
        w = jnp.pad(w, ((0, 0), (0, npad)))
    Np = N + npad
    out = pl.pallas_call(
        _mm_kernel,
        grid=(M // tm,),
        in_specs=[pl.BlockSpec((tm, K), lambda i: (i, 0)),
                  pl.BlockSpec((K, Np), lambda i: (0, 0))],
        out_specs=pl.BlockSpec((tm, Np), lambda i: (i, 0)),
        out_shape=jax.ShapeDtypeStruct((M, Np), jnp.float32),
        compiler_params=pltpu.CompilerParams(dimension_semantics=("arbitrary",),
                                             vmem_limit_bytes=56 * 1024 * 1024),
    )(x, w)
    return out[:, :N] if npad else out


def _mm3(x, w):
    B, T, K = x.shape
    return _matmul(x.reshape(B * T, K), w).reshape(B, T, w.shape[1])


def _layer_norm(x, g, b):
    mu = x.mean(-1, keepdims=True)
    var = jnp.square(x - mu).mean(-1, keepdims=True)
    return (x - mu) * lax.rsqrt(var + LN_EPS) * g + b


def _rms_norm(x, g):
    return x * lax.rsqrt(jnp.mean(x * x, -1, keepdims=True) + 1e-6) * g


def _retention(q, k, v):
    B, T, H, dk = q.shape
    C = RET_CHUNK
    N = T // C
    log_g = jnp.log1p(-jnp.exp2(-5.0 - jnp.arange(H, dtype=jnp.float32)))
    i = jnp.arange(C, dtype=jnp.float32)
    diff = i[:, None] - i[None, :]
    inner_decay = jnp.where(diff >= 0, jnp.exp(log_g[:, None, None] * jnp.maximum(diff, 0.0)), 0.0)
    q_decay = jnp.exp(log_g[:, None] * (i + 1.0))
    k_decay = jnp.exp(log_g[:, None] * (C - 1.0 - i))
    chunk_decay = jnp.exp(log_g * C)

    def to_chunks(a):
        return a.reshape(B, N, C, H, a.shape[-1]).transpose(1, 0, 3, 2, 4)

    def step(R, inp):
        qi, ki, vi = inp
        s = jnp.einsum('bhqd,bhkd->bhqk', qi, ki) * inner_decay
        o = jnp.einsum('bhqk,bhkv->bhqv', s, vi) + jnp.einsum('bhqd,bhdv->bhqv', qi * q_decay[..., None], R)
        R = R * chunk_decay[:, None, None] + jnp.einsum('bhkd,bhkv->bhdv', ki * k_decay[..., None], vi)
        return R, o

    R0 = jnp.zeros((B, H, dk, v.shape[-1]), jnp.float32)
    _, o = lax.scan(step, R0, (to_chunks(q), to_chunks(k), to_chunks(v)))
    return o.transpose(1, 0, 3, 2, 4).reshape(B, T, H, v.shape[-1])


def _pool(u, w_group, scale):
    B, T, _ = u.shape
    nG = len(POOL_WINDOWS)
    uf = u.reshape(B, T, nG, POOL_GROUP)
    c = jnp.concatenate([jnp.zeros((B, 1, nG, POOL_GROUP), jnp.float32), jnp.cumsum(uf, axis=1)], axis=1)
    t = jnp.arange(T)
    means = []
    for gi, w in enumerate(POOL_WINDOWS):
        cg = c[:, :, gi]
        lo = jnp.maximum(t + 1 - w, 0)
        cnt = jnp.minimum(t + 1, w).astype(jnp.float32)
        means.append((cg[:, 1:] - cg[:, lo]) / cnt[None, :, None])
    d = jnp.stack(means, axis=2) - uf
    y = jnp.einsum('btgc,gcd->btgd', d, w_group) * scale.reshape(nG, POOL_GROUP)
    return y.reshape(B, T, nG * POOL_GROUP)


def _mixer_ab(x, w_in, gn_g, pool_w, pool_scale, w_out):
    B, T, _ = x.shape
    nq = RET_HEADS * RET_DK
    nv = RET_HEADS * RET_DV
    z = _mm3(x, w_in)
    q, k, v, g, u = jnp.split(z, [nq, 2 * nq, 2 * nq + nv, 2 * nq + 2 * nv], axis=-1)
    q = q.reshape(B, T, RET_HEADS, RET_DK)
    k = k.reshape(B, T, RET_HEADS, RET_DK) * (RET_DK ** -0.5)
    v = v.reshape(B, T, RET_HEADS, RET_DV)
    o = _retention(q, k, v)
    mu = o.mean(-1, keepdims=True)
    var = jnp.square(o - mu).mean(-1, keepdims=True)
    o = ((o - mu) * lax.rsqrt(var + LN_EPS)).reshape(B, T, nv) * gn_g
    ret = jax.nn.silu(g) * o
    pool = _pool(u, pool_w, pool_scale)
    mixed = jnp.concatenate([ret, pool], axis=-1)
    return _mm3(mixed, w_out)


LANES = 128
QB = Q_BLOCK
DSA_TK = 512
NEG_BIG = -1e30
KEY_LO0 = -2139095040
KEY_HI0 = 2139095040
LOG2E = 1.4426950408889634
BIG_IDX = 1 << 30


def _sum_lane_chunks(v, width):
    out = v[:, 0:LANES]
    for c in range(1, width // LANES):
        out = out + v[:, c * LANES:(c + 1) * LANES]
    return out


def _dsa_kernel(slopes_ref, cq_ref, qidx_ref, widx_ref, kidxT_ref, ckvT_ref, ckv_ref,
                wuq_ref, wukT_ref, wuvp_ref, o_ref,
                keys_scr, wb_scr, q_scr, ls_scr, ps_scr, a_scr, m_scr, l_scr, al_scr, acc_scr,
                th_scr, c_scr, *, topk, seq_len):
    f32 = jnp.float32
    TK = DSA_TK
    NCH = TK // LANES
    H = C_HEADS
    qb = pl.program_id(1)
    t0 = qb * QB
    n_kt = (t0 + QB - 1) // TK + 1
    row = lax.broadcasted_iota(jnp.int32, (QB, TK), 0)
    col = lax.broadcasted_iota(jnp.int32, (QB, TK), 1)
    kf = float(topk)

    for h in range(IDX_HEADS):
        wb_scr[h] = jnp.broadcast_to(widx_ref[0, :, h:h + 1], (QB, LANES))

    def idx_body(j, carry):
        kT = kidxT_ref[0, j]
        acc = jnp.zeros((QB, TK), f32)
        for h in range(IDX_HEADS):
            s = jnp.dot(qidx_ref[0, h], kT, preferred_element_type=f32)
            acc = acc + pltpu.repeat(wb_scr[h], NCH, axis=1) * jnp.maximum(s, 0.0)
        causal = (j * TK + col) <= (t0 + row)
        sc = jnp.where(causal, acc, -jnp.inf)
        b = pltpu.bitcast(sc, jnp.int32)
        keys_scr[j] = b ^ ((b >> 31) & 0x7FFFFFFF)
        return carry

    lax.fori_loop(0, n_kt, idx_body, 0)

    def count_where(ones_fn):
        def body(j, c):
            return c + _sum_lane_chunks(ones_fn(j, keys_scr[j]), TK)
        c = lax.fori_loop(0, n_kt, body, jnp.zeros((QB, LANES), f32))
        return jnp.broadcast_to(jnp.sum(c, axis=1, keepdims=True), (QB, LANES))

    def bis_body(i, carry):
        lo, hi = carry
        mid = (lo | hi) - ((lo ^ hi) >> 1)
        midb = pltpu.repeat(mid, NCH, axis=1)
        ok = count_where(lambda j, k: jnp.where(k >= midb, 1.0, 0.0)) >= kf
        return jnp.where(ok, mid, lo), jnp.where(ok, hi, mid - 1)

    lo0 = jnp.full((QB, LANES), KEY_LO0, jnp.int32)
    hi0 = jnp.full((QB, LANES), KEY_HI0, jnp.int32)
    theta, _ = lax.fori_loop(0, 32, bis_body, (lo0, hi0))
    th_scr[...] = theta
    thb = pltpu.repeat(theta, NCH, axis=1)

    n_gt = count_where(lambda j, k: jnp.where(k > thb, 1.0, 0.0))
    n_ge = count_where(lambda j, k: jnp.where(k >= thb, 1.0, 0.0))
    need = kf - n_gt
    excess = n_ge > kf
    c_scr[...] = jnp.full((QB, LANES), BIG_IDX, jnp.int32)

    @pl.when(jnp.max(jnp.where(excess, 1.0, 0.0)) > 0.0)
    def _():
        def tie_body(i, carry):
            lo, hi = carry
            mid = (lo + hi) >> 1
            midb = pltpu.repeat(mid, NCH, axis=1)
            cnt = count_where(lambda j, k: jnp.where(
                k == thb, jnp.where((j * TK + col) <= midb, 1.0, 0.0), 0.0))
            ok = cnt >= need
            return jnp.where(ok, lo, mid + 1), jnp.where(ok, mid, hi)
        nbits = max(1, (seq_len - 1).bit_length())
        lo, _ = lax.fori_loop(0, nbits, tie_body,
                              (jnp.zeros((QB, LANES), jnp.int32),
                               jnp.full((QB, LANES), seq_len - 1, jnp.int32)))
        c_scr[...] = jnp.where(excess, lo, BIG_IDX)

    cq = cq_ref[0]
    for h in range(H):
        qh = jnp.dot(cq, wuq_ref[h], preferred_element_type=f32).astype(jnp.bfloat16)
        qa = jnp.dot(qh, wukT_ref[h], preferred_element_type=f32) * (C_HEAD_DIM ** -0.5 * LOG2E)
        q_scr[h * QB:(h + 1) * QB, :] = qa.astype(jnp.bfloat16)

    m_scr[...] = jnp.full(m_scr.shape, NEG_BIG, f32)
    l_scr[...] = jnp.zeros(l_scr.shape, f32)
    acc_scr[...] = jnp.zeros(acc_scr.shape, f32)

    def att_body(j, carry):
        ls_scr[...] = jnp.dot(q_scr[...], ckvT_ref[0, j], preferred_element_type=f32)
        k = keys_scr[j]
        thb_ = pltpu.repeat(th_scr[...], NCH, axis=1)
        cb_ = pltpu.repeat(c_scr[...], NCH, axis=1)
        spos = j * TK + col
        keep_tie = jnp.where(spos <= cb_, 0.0, NEG_BIG)
        a_scr[...] = jnp.where(k > thb_, 0.0, jnp.where(k == thb_, keep_tie, NEG_BIG))
        srel8 = (j * TK - t0 + lax.broadcasted_iota(jnp.int32, (8, TK), 1)).astype(f32)

        def head_body(h, c2):
            r0 = pl.multiple_of(h * QB, QB)
            bh = pltpu.repeat(slopes_ref[h] * srel8, QB // 8, axis=0)
            l = ls_scr[pl.ds(r0, QB), :] + a_scr[...] + bh
            m_old = m_scr[pl.ds(r0, QB), :]
            m_new = jnp.maximum(m_old, jnp.max(l, axis=1, keepdims=True))
            p = jnp.exp2(l - pltpu.repeat(m_new, NCH, axis=1))
            alpha = jnp.exp2(m_old - m_new)
            rs = jnp.sum(_sum_lane_chunks(p, TK), axis=1, keepdims=True)
            l_scr[pl.ds(r0, QB), :] = alpha * l_scr[pl.ds(r0, QB), :] + rs
            m_scr[pl.ds(r0, QB), :] = m_new
            al_scr[pl.ds(r0, QB), :] = alpha
            ps_scr[pl.ds(r0, QB), :] = p.astype(jnp.bfloat16)
            return c2

        lax.fori_loop(0, H, head_body, 0)
        pv = jnp.dot(ps_scr[...], ckv_ref[0, j], preferred_element_type=f32)
        acc_scr[...] = acc_scr[...] * pltpu.repeat(al_scr[...], C_KV_RANK // LANES, axis=1) + pv
        return carry

    lax.fori_loop(0, n_kt, att_body, 0)

    for g in range(H // 2):
        parts = []
        for hh in (2 * g, 2 * g + 1):
            inv = 1.0 / l_scr[hh * QB:(hh + 1) * QB, :]
            ol = acc_scr[hh * QB:(hh + 1) * QB, :] * pltpu.repeat(inv, C_KV_RANK // LANES, axis=1)
            parts.append(jnp.dot(ol.astype(jnp.bfloat16), wuvp_ref[hh], preferred_element_type=f32))
        o_ref[0, :, g * LANES:(g + 1) * LANES] = parts[0] + parts[1]


def _dsa_attention(c_q, c_kv, q_idx, k_idx, w_idx, w_uq, w_uk, w_uv):
    B, T, _ = c_q.shape
    topk = min(IDX_TOPK_MAX, T // 4)
    TK = DSA_TK
    NT = T // TK
    H = C_HEADS
    bf = jnp.bfloat16
    slopes2 = jnp.exp2(-8.0 * jnp.arange(1, H + 1, dtype=jnp.float32) / H) * LOG2E
    cq = c_q.astype(bf)
    qidx = q_idx.reshape(B, T, IDX_HEADS, IDX_DIM).transpose(0, 2, 1, 3).astype(bf)
    kidxT = k_idx.astype(bf).reshape(B, NT, TK, IDX_DIM).transpose(0, 1, 3, 2)
    ckv = c_kv.astype(bf).reshape(B, NT, TK, C_KV_RANK)
    ckvT = ckv.transpose(0, 1, 3, 2)
    wuq = w_uq.reshape(C_Q_RANK, H, C_HEAD_DIM).transpose(1, 0, 2).astype(bf)
    wukT = w_uk.transpose(1, 2, 0).astype(bf)
    wuv = w_uv.transpose(1, 0, 2)
    zeros = jnp.zeros_like(wuv)
    even = (jnp.arange(H) % 2 == 0)[:, None, None]
    wuvp = jnp.where(even, jnp.concatenate([wuv, zeros], -1), jnp.concatenate([zeros, wuv], -1)).astype(bf)

    kern = functools.partial(_dsa_kernel, topk=topk, seq_len=T)
    full = lambda shape: pl.BlockSpec(shape, lambda b, q, *_: (0,) * len(shape))
    grid_spec = pltpu.PrefetchScalarGridSpec(
        num_scalar_prefetch=0,
        grid=(B, T // QB),
        in_specs=[
            pl.BlockSpec(memory_space=pltpu.SMEM),
            pl.BlockSpec((1, QB, C_Q_RANK), lambda b, q: (b, q, 0)),
            pl.BlockSpec((1, IDX_HEADS, QB, IDX_DIM), lambda b, q: (b, 0, q, 0)),
            pl.BlockSpec((1, QB, IDX_HEADS), lambda b, q: (b, q, 0)),
            pl.BlockSpec((1, NT, IDX_DIM, TK), lambda b, q: (b, 0, 0, 0)),
            pl.BlockSpec((1, NT, C_KV_RANK, TK), lambda b, q: (b, 0, 0, 0)),
            pl.BlockSpec((1, NT, TK, C_KV_RANK), lambda b, q: (b, 0, 0, 0)),
            full((H, C_Q_RANK, C_HEAD_DIM)),
            full((H, C_HEAD_DIM, C_KV_RANK)),
            full((H, C_KV_RANK, LANES)),
        ],
        out_specs=pl.BlockSpec((1, QB, H * C_HEAD_DIM), lambda b, q: (b, q, 0)),
        scratch_shapes=[
            pltpu.VMEM((NT, QB, TK), jnp.int32),
            pltpu.VMEM((IDX_HEADS, QB, LANES), jnp.float32),
            pltpu.VMEM((H * QB, C_KV_RANK), bf),
            pltpu.VMEM((H * QB, TK), jnp.float32),
            pltpu.VMEM((H * QB, TK), bf),
            pltpu.VMEM((QB, TK), jnp.float32),
            pltpu.VMEM((H * QB, LANES), jnp.float32),
            pltpu.VMEM((H * QB, LANES), jnp.float32),
            pltpu.VMEM((H * QB, LANES), jnp.float32),
            pltpu.VMEM((H * QB, C_KV_RANK), jnp.float32),
            pltpu.VMEM((QB, LANES), jnp.int32),
            pltpu.VMEM((QB, LANES), jnp.int32),
        ],
    )
    return pl.pallas_call(
        kern,
        grid_spec=grid_spec,
        out_shape=jax.ShapeDtypeStruct((B, T, H * C_HEAD_DIM), jnp.float32),
        compiler_params=pltpu.CompilerParams(dimension_semantics=("arbitrary", "arbitrary"),
                                             vmem_limit_bytes=56 * 1024 * 1024),
        name="dsa_attention",
    )(slopes2, cq, qidx, w_idx, kidxT, ckvT, ckv, wuq, wukT, wuvp)


def _mixer_c(x, w_in, q_norm, kv_norm, w_uq, w_uk, w_uv, w_out):
    B, T, _ = x.shape
    z = _mm3(x, w_in)
    s1 = C_Q_RANK
    s2 = s1 + C_KV_RANK
    s3 = s2 + IDX_HEADS * IDX_DIM
    s4 = s3 + IDX_DIM
    c_q, c_kv, q_idx, k_idx, w_idx = jnp.split(z, [s1, s2, s3, s4], axis=-1)
    c_q = _rms_norm(c_q, q_norm)
    c_kv = _rms_norm(c_kv, kv_norm)
    o = _dsa_attention(c_q, c_kv, q_idx, k_idx, w_idx, w_uq, w_uk, w_uv)
    return _mm3(o, w_out)


def _moe(x, w_group, b_group, w_expert, b_expert, w_gu, w_down):
    B, T, D = x.shape
    N = B * T
    xt = x.reshape(N, D)
    ar = jnp.arange(N)
    g_logits = xt @ w_group + b_group
    g_prob = jax.nn.softmax(g_logits, axis=-1)
    _, grp = lax.top_k(g_logits, 1)
    grp = grp[:, 0]
    g_w = g_prob[ar, grp]
    e_logits = (xt @ w_expert + b_expert).reshape(N, N_GROUPS, EXPERTS_PER_GROUP)
    e_logits = e_logits[ar, grp]
    top_v, top_i = lax.top_k(e_logits, TOPK_IN_GROUP)
    gate = g_w[:, None] * jax.nn.softmax(top_v, axis=-1)
    expert = grp[:, None] * EXPERTS_PER_GROUP + top_i
    M = N * TOPK_IN_GROUP
    flat_e = expert.reshape(M)
    flat_tok = jnp.repeat(ar, TOPK_IN_GROUP)
    flat_gate = gate.reshape(M)
    order = jnp.argsort(flat_e)
    e_sorted = flat_e[order]
    counts = jnp.bincount(flat_e, length=N_EXPERTS)
    start = jnp.cumsum(counts) - counts
    padded = ((counts + MOE_BLOCK - 1) // MOE_BLOCK) * MOE_BLOCK
    pend = jnp.cumsum(padded)
    pstart = pend - padded
    dest = pstart[e_sorted] + (jnp.arange(M) - start[e_sorted])
    n_blocks = -(-(M + N_EXPERTS * (MOE_BLOCK - 1)) // MOE_BLOCK)
    R = n_blocks * MOE_BLOCK
    row_tok = jnp.zeros((R,), jnp.int32).at[dest].set(flat_tok[order].astype(jnp.int32))
    row_gate = jnp.zeros((R,), jnp.float32).at[dest].set(flat_gate[order])
    blk_e = jnp.minimum(jnp.searchsorted(pend, jnp.arange(n_blocks) * MOE_BLOCK, side='right'), N_EXPERTS - 1)

    def run(args):
        tok, e = args
        h = xt[tok] @ w_gu[e]
        a, b = jnp.split(h, 2, axis=-1)
        return (jax.nn.silu(a) * b) @ w_down[e]

    ys = lax.map(run, (row_tok.reshape(n_blocks, MOE_BLOCK), blk_e)).reshape(R, D)
    y = jnp.zeros((N, D), jnp.float32).at[row_tok].add(ys * row_gate[:, None])
    return y.reshape(B, T, D)


def kernel(x, p, ab_w_in, ret_gn_g, pool_w, pool_scale, ab_w_out, c_w_in, c_q_norm, c_kv_norm, c_w_uq, c_w_uk, c_w_uv, c_w_out, ln1_g, ln1_b, ln2_g, ln2_b, moe_w_group, moe_b_group, moe_w_expert, moe_b_expert, moe_w_gu, moe_w_down, ple_w_gate, ple_w_proj):
    for i in range(DEPTH):
        j = i // 2
        if i % 2 == 0:
            h = _mixer_ab(x, ab_w_in[j], ret_gn_g[j], pool_w[j], pool_scale[j], ab_w_out[j])
        else:
            h = _mixer_c(x, c_w_in[j], c_q_norm[j], c_kv_norm[j], c_w_uq[j], c_w_uk[j], c_w_uv[j], c_w_out[j])
        x = _layer_norm(ALPHA * x + h, ln1_g[i], ln1_b[i])
        m = _moe(x, moe_w_group[i], moe_b_group[i], moe_w_expert[i], moe_b_expert[i],
                 moe_w_gu[i], moe_w_down[i])
        x = _layer_norm(ALPHA * x + m, ln2_g[i], ln2_b[i])
        x = x + jax.nn.sigmoid(_mm3(x, ple_w_gate[i])) * _mm3(p[i], ple_w_proj[i])
    return x
```

```python
import functools
import math
import jax
import jax.numpy as jnp
from jax import lax
from jax.experimental import pallas as pl
from jax.experimental.pallas import tpu as pltpu

D_MODEL = 1024
DEPTH = 2
RET_HEADS = 4
RET_DK = 128
RET_DV = 128
RET_CHUNK = 128
POOL_WINDOWS = (2, 4, 8, 16)
POOL_GROUP = 128
C_HEADS = 16
C_HEAD_DIM = 64
C_Q_RANK = 256
C_KV_RANK = 256
IDX_HEADS = 8
IDX_DIM = 64
IDX_TOPK_MAX = 256
Q_BLOCK = 128
N_GROUPS = 4
EXPERTS_PER_GROUP = 8
N_EXPERTS = N_GROUPS * EXPERTS_PER_GROUP
TOPK_IN_GROUP = 2
MOE_BLOCK = 128
ALPHA = (2.0 * DEPTH) ** 0.25
LN_EPS = 1e-5


def _mm_kernel(x_ref, w_ref, o_ref):
    o_ref[...] = jnp.dot(x_ref[...].astype(jnp.bfloat16), w_ref[...].astype(jnp.bfloat16),
                         preferred_element_type=jnp.float32)


def _matmul(x, w, tm=512):
    M, K = x.shape
    N = w.shape[1]
    npad = (-N) % 128
    if npad:
        w = jnp.pad(w, ((0, 0), (0, npad)))
    Np = N + npad
    out = pl.pallas_call(
        _mm_kernel,
        grid=(M // tm,),
        in_specs=[pl.BlockSpec((tm, K), lambda i: (i, 0)),
                  pl.BlockSpec((K, Np), lambda i: (0, 0))],
        out_specs=pl.BlockSpec((tm, Np), lambda i: (i, 0)),
        out_shape=jax.ShapeDtypeStruct((M, Np), jnp.float32),
        compiler_params=pltpu.CompilerParams(dimension_semantics=("arbitrary",),
                                             vmem_limit_bytes=56 * 1024 * 1024),
    )(x, w)
    return out[:, :N] if npad else out


def _mm3(x, w):
    B, T, K = x.shape
    return _matmul(x.reshape(B * T, K), w).reshape(B, T, w.shape[1])


def _layer_norm(x, g, b):
    mu = x.mean(-1, keepdims=True)
    var = jnp.square(x - mu).mean(-1, keepdims=True)
    return (x - mu) * lax.rsqrt(var + LN_EPS) * g + b


def _rms_norm(x, g):
    return x * lax.rsqrt(jnp.mean(x * x, -1, keepdims=True) + 1e-6) * g


def _retention(q, k, v):
    B, T, H, dk = q.shape
    C = RET_CHUNK
    N = T // C
    log_g = jnp.log1p(-jnp.exp2(-5.0 - jnp.arange(H, dtype=jnp.float32)))
    i = jnp.arange(C, dtype=jnp.float32)
    diff = i[:, None] - i[None, :]
    inner_decay = jnp.where(diff >= 0, jnp.exp(log_g[:, None, None] * jnp.maximum(diff, 0.0)), 0.0)
    q_decay = jnp.exp(log_g[:, None] * (i + 1.0))
    k_decay = jnp.exp(log_g[:, None] * (C - 1.0 - i))
    chunk_decay = jnp.exp(log_g * C)

    def to_chunks(a):
        return a.reshape(B, N, C, H, a.shape[-1]).transpose(1, 0, 3, 2, 4)

    def step(R, inp):
        qi, ki, vi = inp
        s = jnp.einsum('bhqd,bhkd->bhqk', qi, ki) * inner_decay
        o = jnp.einsum('bhqk,bhkv->bhqv', s, vi) + jnp.einsum('bhqd,bhdv->bhqv', qi * q_decay[..., None], R)
        R = R * chunk_decay[:, None, None] + jnp.einsum('bhkd,bhkv->bhdv', ki * k_decay[..., None], vi)
        return R, o

    R0 = jnp.zeros((B, H, dk, v.shape[-1]), jnp.float32)
    _, o = lax.scan(step, R0, (to_chunks(q), to_chunks(k), to_chunks(v)))
    return o.transpose(1, 0, 3, 2, 4).reshape(B, T, H, v.shape[-1])


def _pool(u, w_group, scale):
    B, T, _ = u.shape
    nG = len(POOL_WINDOWS)
    uf = u.reshape(B, T, nG, POOL_GROUP)
    c = jnp.concatenate([jnp.zeros((B, 1, nG, POOL_GROUP), jnp.float32), jnp.cumsum(uf, axis=1)], axis=1)
    t = jnp.arange(T)
    means = []
    for gi, w in enumerate(POOL_WINDOWS):
        cg = c[:, :, gi]
        lo = jnp.maximum(t + 1 - w, 0)
        cnt = jnp.minimum(t + 1, w).astype(jnp.float32)
        means.append((cg[:, 1:] - cg[:, lo]) / cnt[None, :, None])
    d = jnp.stack(means, axis=2) - uf
    y = jnp.einsum('btgc,gcd->btgd', d, w_group) * scale.reshape(nG, POOL_GROUP)
    return y.reshape(B, T, nG * POOL_GROUP)


def _mixer_ab(x, w_in, gn_g, pool_w, pool_scale, w_out):
    B, T, _ = x.shape
    nq = RET_HEADS * RET_DK
    nv = RET_HEADS * RET_DV
    z = _mm3(x, w_in)
    q, k, v, g, u = jnp.split(z, [nq, 2 * nq, 2 * nq + nv, 2 * nq + 2 * nv], axis=-1)
    q = q.reshape(B, T, RET_HEADS, RET_DK)
    k = k.reshape(B, T, RET_HEADS, RET_DK) * (RET_DK ** -0.5)
    v = v.reshape(B, T, RET_HEADS, RET_DV)
    o = _retention(q, k, v)
    mu = o.mean(-1, keepdims=True)
    var = jnp.square(o - mu).mean(-1, keepdims=True)
    o = ((o - mu) * lax.rsqrt(var + LN_EPS)).reshape(B, T, nv) * gn_g
    ret = jax.nn.silu(g) * o
    pool = _pool(u, pool_w, pool_scale)
    mixed = jnp.concatenate([ret, pool], axis=-1)
    return _mm3(mixed, w_out)


LANES = 128
QB = Q_BLOCK
DSA_TK = 512
NEG_BIG = -1e30
KEY_LO0 = -2139095040
KEY_HI0 = 2139095040
LOG2E = 1.4426950408889634
BIG_IDX = 1 << 30


def _rep(x, n, axis):
    return jnp.concatenate([x] * n, axis=axis) if n > 1 else x


def _sum_lane_chunks(v, width):
    out = v[:, 0:LANES]
    for c in range(1, width // LANES):
        out = out + v[:, c * LANES:(c + 1) * LANES]
    return out


def _dsa_kernel(slopes_ref, cq_ref, qidx_ref, widx_ref, kidxT_ref, ckvT_ref, ckv_ref,
                wuq_ref, wukT_ref, wuvp_ref, o_ref,
                keys_scr, wb_scr, q_scr, ls_scr, ps_scr, a_scr, m_scr, l_scr, al_scr, acc_scr,
                th_scr, c_scr, *, topk, seq_len):
    f32 = jnp.float32
    TK = DSA_TK
    NCH = TK // LANES
    H = C_HEADS
    qb = pl.program_id(1)
    t0 = qb * QB
    n_kt = (t0 + QB - 1) // TK + 1
    row = lax.broadcasted_iota(jnp.int32, (QB, TK), 0)
    col = lax.broadcasted_iota(jnp.int32, (QB, TK), 1)
    kf = float(topk)

    for h in range(IDX_HEADS):
        wb_scr[h] = jnp.broadcast_to(widx_ref[0, :, h:h + 1], (QB, LANES))

    def idx_body(j, carry):
        kT = kidxT_ref[0, j]
        acc = jnp.zeros((QB, TK), f32)
        for h in range(IDX_HEADS):
            s = jnp.dot(qidx_ref[0, h], kT, preferred_element_type=f32)
            acc = acc + _rep(wb_scr[h], NCH, axis=1) * jnp.maximum(s, 0.0)
        causal = (j * TK + col) <= (t0 + row)
        sc = jnp.where(causal, acc, -jnp.inf)
        b = pltpu.bitcast(sc, jnp.int32)
        keys_scr[j] = b ^ ((b >> 31) & 0x7FFFFFFF)
        return carry

    lax.fori_loop(0, n_kt, idx_body, 0)

    def count_where(ones_fn):
        def body(j, c):
            return c + _sum_lane_chunks(ones_fn(j, keys_scr[j]), TK)
        c = lax.fori_loop(0, n_kt, body, jnp.zeros((QB, LANES), f32))
        return jnp.broadcast_to(jnp.sum(c, axis=1, keepdims=True), (QB, LANES))

    def bis_body(i, carry):
        lo, hi = carry
        mid = (lo | hi) - ((lo ^ hi) >> 1)
        midb = _rep(mid, NCH, axis=1)
        ok = count_where(lambda j, k: jnp.where(k >= midb, 1.0, 0.0)) >= kf
        return jnp.where(ok, mid, lo), jnp.where(ok, hi, mid - 1)

    lo0 = jnp.full((QB, LANES), KEY_LO0, jnp.int32)
    hi0 = jnp.full((QB, LANES), KEY_HI0, jnp.int32)
    theta, _ = lax.fori_loop(0, 32, bis_body, (lo0, hi0))
    th_scr[...] = theta
    thb = _rep(theta, NCH, axis=1)

    n_gt = count_where(lambda j, k: jnp.where(k > thb, 1.0, 0.0))
    n_ge = count_where(lambda j, k: jnp.where(k >= thb, 1.0, 0.0))
    need = kf - n_gt
    excess = n_ge > kf
    c_scr[...] = jnp.full((QB, LANES), BIG_IDX, jnp.int32)

    @pl.when(jnp.max(jnp.where(excess, 1.0, 0.0)) > 0.0)
    def _():
        def tie_body(i, carry):
            lo, hi = carry
            mid = (lo + hi) >> 1
            midb = _rep(mid, NCH, axis=1)
            cnt = count_where(lambda j, k: jnp.where(
                k == thb, jnp.where((j * TK + col) <= midb, 1.0, 0.0), 0.0))
            ok = cnt >= need
            return jnp.where(ok, lo, mid + 1), jnp.where(ok, mid, hi)
        nbits = max(1, (seq_len - 1).bit_length())
        lo, _ = lax.fori_loop(0, nbits, tie_body,
                              (jnp.zeros((QB, LANES), jnp.int32),
                               jnp.full((QB, LANES), seq_len - 1, jnp.int32)))
        c_scr[...] = jnp.where(excess, lo, BIG_IDX)

    cq = cq_ref[0]
    for h in range(H):
        qh = jnp.dot(cq, wuq_ref[h], preferred_element_type=f32).astype(jnp.bfloat16)
        qa = jnp.dot(qh, wukT_ref[h], preferred_element_type=f32) * (C_HEAD_DIM ** -0.5 * LOG2E)
        q_scr[h * QB:(h + 1) * QB, :] = qa.astype(jnp.bfloat16)

    m_scr[...] = jnp.full(m_scr.shape, NEG_BIG, f32)
    l_scr[...] = jnp.zeros(l_scr.shape, f32)
    acc_scr[...] = jnp.zeros(acc_scr.shape, f32)

    def att_body(j, carry):
        ls_scr[...] = jnp.dot(q_scr[...], ckvT_ref[0, j], preferred_element_type=f32)
        k = keys_scr[j]
        thb_ = _rep(th_scr[...], NCH, axis=1)
        cb_ = _rep(c_scr[...], NCH, axis=1)
        spos = j * TK + col
        keep_tie = jnp.where(spos <= cb_, 0.0, NEG_BIG)
        a_scr[...] = jnp.where(k > thb_, 0.0, jnp.where(k == thb_, keep_tie, NEG_BIG))
        srel8 = (j * TK - t0 + lax.broadcasted_iota(jnp.int32, (8, TK), 1)).astype(f32)

        def head_body(h, c2):
            r0 = pl.multiple_of(h * QB, QB)
            bh = _rep(slopes_ref[h] * srel8, QB // 8, axis=0)
            l = ls_scr[pl.ds(r0, QB), :] + a_scr[...] + bh
            m_old = m_scr[pl.ds(r0, QB), :]
            m_new = jnp.maximum(m_old, jnp.max(l, axis=1, keepdims=True))
            p = jnp.exp2(l - _rep(m_new, NCH, axis=1))
            alpha = jnp.exp2(m_old - m_new)
            rs = jnp.sum(_sum_lane_chunks(p, TK), axis=1, keepdims=True)
            l_scr[pl.ds(r0, QB), :] = alpha * l_scr[pl.ds(r0, QB), :] + rs
            m_scr[pl.ds(r0, QB), :] = m_new
            al_scr[pl.ds(r0, QB), :] = alpha
            ps_scr[pl.ds(r0, QB), :] = p.astype(jnp.bfloat16)
            return c2

        lax.fori_loop(0, H, head_body, 0)
        pv = jnp.dot(ps_scr[...], ckv_ref[0, j], preferred_element_type=f32)
        acc_scr[...] = acc_scr[...] * _rep(al_scr[...], C_KV_RANK // LANES, axis=1) + pv
        return carry

    lax.fori_loop(0, n_kt, att_body, 0)

    for g in range(H // 2):
        parts = []
        for hh in (2 * g, 2 * g + 1):
            inv = 1.0 / l_scr[hh * QB:(hh + 1) * QB, :]
            ol = acc_scr[hh * QB:(hh + 1) * QB, :] * _rep(inv, C_KV_RANK // LANES, axis=1)
            parts.append(jnp.dot(ol.astype(jnp.bfloat16), wuvp_ref[hh], preferred_element_type=f32))
        o_ref[0, :, g * LANES:(g + 1) * LANES] = parts[0] + parts[1]


def _dsa_attention(c_q, c_kv, q_idx, k_idx, w_idx, w_uq, w_uk, w_uv):
    B, T, _ = c_q.shape
    topk = min(IDX_TOPK_MAX, T // 4)
    TK = DSA_TK
    NT = T // TK
    H = C_HEADS
    bf = jnp.bfloat16
    slopes2 = jnp.exp2(-8.0 * jnp.arange(1, H + 1, dtype=jnp.float32) / H) * LOG2E
    cq = c_q.astype(bf)
    qidx = q_idx.reshape(B, T, IDX_HEADS, IDX_DIM).transpose(0, 2, 1, 3).astype(bf)
    kidxT = k_idx.astype(bf).reshape(B, NT, TK, IDX_DIM).transpose(0, 1, 3, 2)
    ckv = c_kv.astype(bf).reshape(B, NT, TK, C_KV_RANK)
    ckvT = ckv.transpose(0, 1, 3, 2)
    wuq = w_uq.reshape(C_Q_RANK, H, C_HEAD_DIM).transpose(1, 0, 2).astype(bf)
    wukT = w_uk.transpose(1, 2, 0).astype(bf)
    wuv = w_uv.transpose(1, 0, 2)
    zeros = jnp.zeros_like(wuv)
    even = (jnp.arange(H) % 2 == 0)[:, None, None]
    wuvp = jnp.where(even, jnp.concatenate([wuv, zeros], -1), jnp.concatenate([zeros, wuv], -1)).astype(bf)

    kern = functools.partial(_dsa_kernel, topk=topk, seq_len=T)
    full = lambda shape: pl.BlockSpec(shape, lambda b, q, *_: (0,) * len(shape))
    grid_spec = pltpu.PrefetchScalarGridSpec(
        num_scalar_prefetch=0,
        grid=(B, T // QB),
        in_specs=[
            pl.BlockSpec(memory_space=pltpu.SMEM),
            pl.BlockSpec((1, QB, C_Q_RANK), lambda b, q: (b, q, 0)),
            pl.BlockSpec((1, IDX_HEADS, QB, IDX_DIM), lambda b, q: (b, 0, q, 0)),
            pl.BlockSpec((1, QB, IDX_HEADS), lambda b, q: (b, q, 0)),
            pl.BlockSpec((1, NT, IDX_DIM, TK), lambda b, q: (b, 0, 0, 0)),
            pl.BlockSpec((1, NT, C_KV_RANK, TK), lambda b, q: (b, 0, 0, 0)),
            pl.BlockSpec((1, NT, TK, C_KV_RANK), lambda b, q: (b, 0, 0, 0)),
            full((H, C_Q_RANK, C_HEAD_DIM)),
            full((H, C_HEAD_DIM, C_KV_RANK)),
            full((H, C_KV_RANK, LANES)),
        ],
        out_specs=pl.BlockSpec((1, QB, H * C_HEAD_DIM), lambda b, q: (b, q, 0)),
        scratch_shapes=[
            pltpu.VMEM((NT, QB, TK), jnp.int32),
            pltpu.VMEM((IDX_HEADS, QB, LANES), jnp.float32),
            pltpu.VMEM((H * QB, C_KV_RANK), bf),
            pltpu.VMEM((H * QB, TK), jnp.float32),
            pltpu.VMEM((H * QB, TK), bf),
            pltpu.VMEM((QB, TK), jnp.float32),
            pltpu.VMEM((H * QB, LANES), jnp.float32),
            pltpu.VMEM((H * QB, LANES), jnp.float32),
            pltpu.VMEM((H * QB, LANES), jnp.float32),
            pltpu.VMEM((H * QB, C_KV_RANK), jnp.float32),
            pltpu.VMEM((QB, LANES), jnp.int32),
            pltpu.VMEM((QB, LANES), jnp.int32),
        ],
    )
    return pl.pallas_call(
        kern,
        grid_spec=grid_spec,
        out_shape=jax.ShapeDtypeStruct((B, T, H * C_HEAD_DIM), jnp.float32),
        compiler_params=pltpu.CompilerParams(dimension_semantics=("arbitrary", "arbitrary"),
                                             vmem_limit_bytes=56 * 1024 * 1024),
        name="dsa_attention",
    )(slopes2, cq, qidx, w_idx, kidxT, ckvT, ckv, wuq, wukT, wuvp)


def _mixer_c(x, w_in, q_norm, kv_norm, w_uq, w_uk, w_uv, w_out):
    B, T, _ = x.shape
    z = _mm3(x, w_in)
    s1 = C_Q_RANK
    s2 = s1 + C_KV_RANK
    s3 = s2 + IDX_HEADS * IDX_DIM
    s4 = s3 + IDX_DIM
    c_q, c_kv, q_idx, k_idx, w_idx = jnp.split(z, [s1, s2, s3, s4], axis=-1)
    c_q = _rms_norm(c_q, q_norm)
    c_kv = _rms_norm(c_kv, kv_norm)
    o = _dsa_attention(c_q, c_kv, q_idx, k_idx, w_idx, w_uq, w_uk, w_uv)
    return _mm3(o, w_out)


def _moe(x, w_group, b_group, w_expert, b_expert, w_gu, w_down):
    B, T, D = x.shape
    N = B * T
    xt = x.reshape(N, D)
    ar = jnp.arange(N)
    g_logits = xt @ w_group + b_group
    g_prob = jax.nn.softmax(g_logits, axis=-1)
    _, grp = lax.top_k(g_logits, 1)
    grp = grp[:, 0]
    g_w = g_prob[ar, grp]
    e_logits = (xt @ w_expert + b_expert).reshape(N, N_GROUPS, EXPERTS_PER_GROUP)
    e_logits = e_logits[ar, grp]
    top_v, top_i = lax.top_k(e_logits, TOPK_IN_GROUP)
    gate = g_w[:, None] * jax.nn.softmax(top_v, axis=-1)
    expert = grp[:, None] * EXPERTS_PER_GROUP + top_i
    M = N * TOPK_IN_GROUP
    flat_e = expert.reshape(M).astype(jnp.int32)
    oh = (flat_e[:, None] == jnp.arange(N_EXPERTS, dtype=jnp.int32)[None, :]).astype(jnp.int32)
    csum = jnp.cumsum(oh, axis=0)
    rank = jnp.take_along_axis(csum, flat_e[:, None], axis=1)[:, 0] - 1
    counts = csum[-1]
    y = _moe_experts(xt, flat_e, rank, counts, gate, w_gu, w_down)
    return y.reshape(B, T, D)


MOE_BLK = 256
MOE_TM = 256


def _dispatch_kernel(dest_ref, x_ref, xs_init_ref, xs_ref, sem):
    del xs_init_ref
    base = pl.program_id(0) * (TOPK_IN_GROUP * MOE_TM)

    def row_copy(t, k):
        d = dest_ref[base + TOPK_IN_GROUP * t + k]
        return pltpu.make_async_copy(x_ref.at[pl.ds(t, 1), :], xs_ref.at[pl.ds(d, 1), :], sem)

    def issue(t, c):
        for k in range(TOPK_IN_GROUP):
            row_copy(t, k).start()
        return c

    def drain(t, c):
        for k in range(TOPK_IN_GROUP):
            row_copy(t, k).wait()
        return c

    lax.fori_loop(0, MOE_TM, issue, 0)
    lax.fori_loop(0, MOE_TM, drain, 0)


def _expert_kernel(be_ref, nu_ref, xs_ref, wgu_ref, wd_ref, ys_ref, wgu_bf, wd_bf):
    b = pl.program_id(0)
    F = wd_bf.shape[0]

    @pl.when(b < nu_ref[0])
    def _():
        e = be_ref[b]
        prev = be_ref[jnp.maximum(b - 1, 0)]

        @pl.when((b == 0) | (e != prev))
        def _():
            wgu_bf[...] = wgu_ref[0].astype(jnp.bfloat16)
            wd_bf[...] = wd_ref[0].astype(jnp.bfloat16)

        h = jnp.dot(xs_ref[...].astype(jnp.bfloat16), wgu_bf[...], preferred_element_type=jnp.float32)
        a = h[:, :F]
        act = a * jax.nn.sigmoid(a) * h[:, F:]
        ys_ref[...] = jnp.dot(act.astype(jnp.bfloat16), wd_bf[...], preferred_element_type=jnp.float32)

    @pl.when(b >= nu_ref[0])
    def _():
        ys_ref[...] = jnp.zeros(ys_ref.shape, ys_ref.dtype)


def _combine_kernel(dest_ref, ys_ref, gate_ref, o_ref, buf, sem):
    i = pl.program_id(0)
    n = pl.num_programs(0)

    def row_copy(step, slot, t, k):
        d = dest_ref[step * (TOPK_IN_GROUP * MOE_TM) + TOPK_IN_GROUP * t + k]
        return pltpu.make_async_copy(ys_ref.at[pl.ds(d, 1), :], buf.at[slot, k, pl.ds(t, 1), :], sem.at[slot])

    def issue(step, slot):
        def body(t, c):
            for k in range(TOPK_IN_GROUP):
                row_copy(step, slot, t, k).start()
            return c
        lax.fori_loop(0, MOE_TM, body, 0)

    def drain(step, slot):
        def body(t, c):
            for k in range(TOPK_IN_GROUP):
                row_copy(step, slot, t, k).wait()
            return c
        lax.fori_loop(0, MOE_TM, body, 0)

    @pl.when(i == 0)
    def _():
        issue(0, 0)

    @pl.when(i + 1 < n)
    def _():
        issue(i + 1, (i + 1) % 2)

    slot = i % 2
    drain(i, slot)
    g = gate_ref[...]
    o_ref[...] = g[:, 0:1] * buf[slot, 0] + g[:, 1:2] * buf[slot, 1]


def _moe_experts(xt, flat_e, rank, counts, gate, w_gu, w_down):
    N, D = xt.shape
    M = flat_e.shape[0]
    F = w_down.shape[1]
    BLK, TM = MOE_BLK, MOE_TM
    nb = -(-(M + N_EXPERTS * (BLK - 1)) // BLK)
    R = nb * BLK
    padded = ((counts + BLK - 1) // BLK) * BLK
    pend = jnp.cumsum(padded)
    pstart = pend - padded
    dest = (pstart[flat_e] + rank).astype(jnp.int32)
    nused = (pend[-1:] // BLK).astype(jnp.int32)
    blk_e = jnp.minimum(jnp.searchsorted(pend, jnp.arange(nb) * BLK, side='right'),
                        N_EXPERTS - 1).astype(jnp.int32)
    vmem = 56 * 1024 * 1024

    xs = pl.pallas_call(
        _dispatch_kernel,
        grid_spec=pltpu.PrefetchScalarGridSpec(
            num_scalar_prefetch=1,
            grid=(N // TM,),
            in_specs=[pl.BlockSpec((TM, D), lambda i, d: (i, 0)),
                      pl.BlockSpec(memory_space=pl.ANY)],
            out_specs=pl.BlockSpec(memory_space=pl.ANY),
            scratch_shapes=[pltpu.SemaphoreType.DMA(())],
        ),
        out_shape=jax.ShapeDtypeStruct((R, D), jnp.float32),
        input_output_aliases={2: 0},
        compiler_params=pltpu.CompilerParams(dimension_semantics=("arbitrary",), vmem_limit_bytes=vmem),
        name="moe_dispatch",
    )(dest, xt, jnp.zeros((R, D), jnp.float32))

    def blk(b, be, nu):
        return jnp.minimum(b, nu[0] - 1)

    ys = pl.pallas_call(
        _expert_kernel,
        grid_spec=pltpu.PrefetchScalarGridSpec(
            num_scalar_prefetch=2,
            grid=(nb,),
            in_specs=[pl.BlockSpec((BLK, D), lambda b, be, nu: (blk(b, be, nu), 0)),
                      pl.BlockSpec((1, D, 2 * F), lambda b, be, nu: (be[blk(b, be, nu)], 0, 0)),
                      pl.BlockSpec((1, F, D), lambda b, be, nu: (be[blk(b, be, nu)], 0, 0))],
            out_specs=pl.BlockSpec((BLK, D), lambda b, be, nu: (b, 0)),
            scratch_shapes=[pltpu.VMEM((D, 2 * F), jnp.bfloat16), pltpu.VMEM((F, D), jnp.bfloat16)],
        ),
        out_shape=jax.ShapeDtypeStruct((R, D), jnp.float32),
        compiler_params=pltpu.CompilerParams(dimension_semantics=("arbitrary",), vmem_limit_bytes=vmem),
        name="moe_experts",
    )(blk_e, nused, xs, w_gu, w_down)

    return pl.pallas_call(
        _combine_kernel,
        grid_spec=pltpu.PrefetchScalarGridSpec(
            num_scalar_prefetch=1,
            grid=(N // TM,),
            in_specs=[pl.BlockSpec(memory_space=pl.ANY),
                      pl.BlockSpec((TM, TOPK_IN_GROUP), lambda i, d: (i, 0))],
            out_specs=pl.BlockSpec((TM, D), lambda i, d: (i, 0)),
            scratch_shapes=[pltpu.VMEM((2, TOPK_IN_GROUP, TM, D), jnp.float32),
                            pltpu.SemaphoreType.DMA((2,))],
        ),
        out_shape=jax.ShapeDtypeStruct((N, D), jnp.float32),
        compiler_params=pltpu.CompilerParams(dimension_semantics=("arbitrary",), vmem_limit_bytes=vmem),
        name="moe_combine",
    )(dest, ys, gate)


def kernel(x, p, ab_w_in, ret_gn_g, pool_w, pool_scale, ab_w_out, c_w_in, c_q_norm, c_kv_norm, c_w_uq, c_w_uk, c_w_uv, c_w_out, ln1_g, ln1_b, ln2_g, ln2_b, moe_w_group, moe_b_group, moe_w_expert, moe_b_expert, moe_w_gu, moe_w_down, ple_w_gate, ple_w_proj):
    for i in range(DEPTH):
        j = i // 2
        if i % 2 == 0:
            h = _mixer_ab(x, ab_w_in[j], ret_gn_g[j], pool_w[j], pool_scale[j], ab_w_out[j])
        else:
            h = _mixer_c(x, c_w_in[j], c_q_norm[j], c_kv_norm[j], c_w_uq[j], c_w_uk[j], c_w_uv[j], c_w_out[j])
        x = _layer_norm(ALPHA * x + h, ln1_g[i], ln1_b[i])
        m = _moe(x, moe_w_group[i], moe_b_group[i], moe_w_expert[i], moe_b_expert[i],
                 moe_w_gu[i], moe_w_down[i])
        x = _layer_norm(ALPHA * x + m, ln2_g[i], ln2_b[i])
        x = x + jax.nn.sigmoid(_mm3(x, ple_w_gate[i])) * _mm3(p[i], ple_w_proj[i])
    return x
```

```python
import functools
import math
import jax
import jax.numpy as jnp
from jax import lax
from jax.experimental import pallas as pl
from jax.experimental.pallas import tpu as pltpu

D_MODEL = 1024
DEPTH = 2
RET_HEADS = 4
RET_DK = 128
RET_DV = 128
RET_CHUNK = 128
POOL_WINDOWS = (2, 4, 8, 16)
POOL_GROUP = 128
C_HEADS = 16
C_HEAD_DIM = 64
C_Q_RANK = 256
C_KV_RANK = 256
IDX_HEADS = 8
IDX_DIM = 64
IDX_TOPK_MAX = 256
Q_BLOCK = 128
N_GROUPS = 4
EXPERTS_PER_GROUP = 8
N_EXPERTS = N_GROUPS * EXPERTS_PER_GROUP
TOPK_IN_GROUP = 2
MOE_BLOCK = 128
ALPHA = (2.0 * DEPTH) ** 0.25
LN_EPS = 1e-5


def _mm_kernel(x_ref, w_ref, o_ref):
    o_ref[...] = jnp.dot(x_ref[...].astype(jnp.bfloat16), w_ref[...].astype(jnp.bfloat16),
                         preferred_element_type=jnp.float32)


def _matmul(x, w, tm=512):
    M, K = x.shape
    N = w.shape[1]
    npad = (-N) % 128
    if npad:
        w = jnp.pad(w, ((0, 0), (0, npad)))
    Np = N + npad
    out = pl.pallas_call(
        _mm_kernel,
        grid=(M // tm,),
        in_specs=[pl.BlockSpec((tm, K), lambda i: (i, 0)),
                  pl.BlockSpec((K, Np), lambda i: (0, 0))],
        out_specs=pl.BlockSpec((tm, Np), lambda i: (i, 0)),
        out_shape=jax.ShapeDtypeStruct((M, Np), jnp.float32),
        compiler_params=pltpu.CompilerParams(dimension_semantics=("arbitrary",),
                                             vmem_limit_bytes=56 * 1024 * 1024),
    )(x, w)
    return out[:, :N] if npad else out


def _mm3(x, w):
    B, T, K = x.shape
    return _matmul(x.reshape(B * T, K), w).reshape(B, T, w.shape[1])


def _layer_norm(x, g, b):
    mu = x.mean(-1, keepdims=True)
    var = jnp.square(x - mu).mean(-1, keepdims=True)
    return (x - mu) * lax.rsqrt(var + LN_EPS) * g + b


def _rms_norm(x, g):
    return x * lax.rsqrt(jnp.mean(x * x, -1, keepdims=True) + 1e-6) * g


TOK_TM = 512
POOL_TAIL = 16
VMEM_LIMIT = 56 * 1024 * 1024
ROUTE_W = 128


def _bdot(a, b):
    return jnp.dot(a.astype(jnp.bfloat16), b.astype(jnp.bfloat16), preferred_element_type=jnp.float32)


def _ln_rows(v, g, b):
    mu = jnp.mean(v, axis=1, keepdims=True)
    d = v - mu
    var = jnp.mean(d * d, axis=1, keepdims=True)
    return d * lax.rsqrt(var + LN_EPS) * g + b


def _route_rows(x1, wr_ref, br_ref, tri_ref, run_scr, cnt_ref):
    f32 = jnp.float32
    logits = _bdot(x1, wr_ref[...]) + br_ref[...]
    lane = lax.broadcasted_iota(jnp.int32, logits.shape, 1)
    lane_f = lane.astype(f32)
    ninf = -jnp.inf
    gl = jnp.where(lane < N_GROUPS, logits, ninf)
    gmax = jnp.max(gl, axis=1, keepdims=True)
    grp = jnp.min(jnp.where(gl == gmax, lane_f, float(ROUTE_W)), axis=1, keepdims=True)
    g_w = 1.0 / jnp.sum(jnp.exp(gl - gmax), axis=1, keepdims=True)
    egrp = ((lane - N_GROUPS) >> 3).astype(f32)
    el = jnp.where(lane >= N_GROUPS, jnp.where(egrp == grp, logits, ninf), ninf)
    v1 = jnp.max(el, axis=1, keepdims=True)
    i1 = jnp.min(jnp.where(el == v1, lane_f, float(ROUTE_W)), axis=1, keepdims=True)
    el2 = jnp.where(lane_f == i1, ninf, el)
    v2 = jnp.max(el2, axis=1, keepdims=True)
    i2 = jnp.min(jnp.where(el2 == v2, lane_f, float(ROUTE_W)), axis=1, keepdims=True)
    e = jnp.exp(v2 - v1)
    den = 1.0 / (1.0 + e)
    oh1 = jnp.where(lane_f == i1, 1.0, 0.0)
    oh2 = jnp.where(lane_f == i2, 1.0, 0.0)
    oh = oh1 + oh2
    base = run_scr[0:1, :] + jnp.dot(tri_ref[...], oh.astype(jnp.bfloat16), preferred_element_type=f32)
    r1 = jnp.sum(oh1 * base, axis=1, keepdims=True)
    r2 = jnp.sum(oh2 * base, axis=1, keepdims=True)
    run_new = run_scr[0:1, :] + jnp.sum(oh, axis=0, keepdims=True)
    run_scr[0:1, :] = run_new
    cnt_ref[...] = jnp.broadcast_to(run_new, cnt_ref.shape)
    sel = lambda l, v, rest: jnp.where(lane == l, v, rest)
    return sel(0, i1 - N_GROUPS, sel(1, i2 - N_GROUPS, sel(2, g_w * den, sel(3, g_w * e * den,
                                                                             sel(4, r1, sel(5, r2, 0.0))))))


def _mixer_ab_kernel(cdec_ref, x_ref, win_ref, idec_ref, qdec_ref, kdec_ref, gng_ref, bandc_ref, bandp_ref,
                     pw_ref, psc_ref, wout_ref, lng_ref, lnb_ref, wr_ref, br_ref, tri_ref,
                     x1_ref, route_ref, cnt_ref, r_scr, tail_scr, mixed_scr, run_scr):
    f32 = jnp.float32
    bf = jnp.bfloat16
    C = RET_CHUNK
    TM = x_ref.shape[1]
    nq = RET_HEADS * RET_DK
    nv = RET_HEADS * RET_DV
    ti = pl.program_id(1)

    @pl.when(ti == 0)
    def _():
        r_scr[...] = jnp.zeros(r_scr.shape, f32)
        tail_scr[...] = jnp.zeros(tail_scr.shape, f32)

    x = x_ref[0]
    xb = x.astype(bf)
    zq = jnp.dot(xb, win_ref[:, 0:nq], preferred_element_type=f32)
    zk = jnp.dot(xb, win_ref[:, nq:2 * nq], preferred_element_type=f32) * (RET_DK ** -0.5)
    zv = jnp.dot(xb, win_ref[:, 2 * nq:2 * nq + nv], preferred_element_type=f32)
    zg = jnp.dot(xb, win_ref[:, 2 * nq + nv:2 * nq + 2 * nv], preferred_element_type=f32)
    zu = jnp.dot(xb, win_ref[:, 2 * nq + 2 * nv:], preferred_element_type=f32)

    for h in range(RET_HEADS):
        hs = slice(h * RET_DK, (h + 1) * RET_DK)
        R = r_scr[h]
        for c in range(TM // C):
            cs = slice(c * C, (c + 1) * C)
            qc, kc, vc = zq[cs, hs], zk[cs, hs], zv[cs, hs]
            s = lax.dot_general(qc.astype(bf), kc.astype(bf), (((1,), (1,)), ((), ())),
                                preferred_element_type=f32) * idec_ref[h]
            o = _bdot(s, vc) + _bdot(qc * qdec_ref[h], R)
            R = R * cdec_ref[h] + lax.dot_general((kc * kdec_ref[h]).astype(bf), vc.astype(bf),
                                                  (((0,), (0,)), ((), ())), preferred_element_type=f32)
            mu = jnp.mean(o, axis=1, keepdims=True)
            d = o - mu
            var = jnp.mean(d * d, axis=1, keepdims=True)
            gate = zg[cs, hs]
            ret = d * lax.rsqrt(var + LN_EPS) * gng_ref[:, hs] * (gate * jax.nn.sigmoid(gate))
            mixed_scr[cs, hs] = ret.astype(bf)
        r_scr[h] = R

    t_glob = ti * TM + lax.broadcasted_iota(jnp.int32, (TM, POOL_GROUP), 0)
    tail = tail_scr[...]
    for gi, w in enumerate(POOL_WINDOWS):
        gs = slice(gi * POOL_GROUP, (gi + 1) * POOL_GROUP)
        u = zu[:, gs]
        u_hi = u.astype(bf)
        u_lo = (u - u_hi.astype(f32)).astype(bf)
        tl = tail[:, gs]
        t_hi = tl.astype(bf)
        t_lo = (tl - t_hi.astype(f32)).astype(bf)
        wsum = (jnp.dot(bandc_ref[gi], u_hi, preferred_element_type=f32)
                + jnp.dot(bandc_ref[gi], u_lo, preferred_element_type=f32)
                + jnp.dot(bandp_ref[gi], t_hi, preferred_element_type=f32)
                + jnp.dot(bandp_ref[gi], t_lo, preferred_element_type=f32))
        cnt = jnp.minimum(t_glob + 1, w).astype(f32)
        dpool = wsum / cnt - u
        y = _bdot(dpool, pw_ref[gi]) * psc_ref[:, gs]
        mixed_scr[:, nv + gi * POOL_GROUP:nv + (gi + 1) * POOL_GROUP] = y.astype(bf)
    tail_scr[...] = zu[TM - POOL_TAIL:, :]

    hmix = jnp.dot(mixed_scr[...], wout_ref[...], preferred_element_type=f32)
    x1 = _ln_rows(ALPHA * x + hmix, lng_ref[...], lnb_ref[...])
    x1_ref[0] = x1
    @pl.when((pl.program_id(0) == 0) & (ti == 0))
    def _():
        run_scr[...] = jnp.zeros(run_scr.shape, f32)

    route_ref[0] = _route_rows(x1, wr_ref, br_ref, tri_ref, run_scr, cnt_ref)


def _const_spec(shape):
    return pl.BlockSpec(shape, lambda b, t: (0,) * len(shape))


def _router_weights(w_group, b_group, w_expert, b_expert):
    D = w_group.shape[0]
    pad = ROUTE_W - N_GROUPS - N_EXPERTS
    wr = jnp.concatenate([w_group, w_expert, jnp.zeros((D, pad), jnp.float32)], axis=1).astype(jnp.bfloat16)
    br = jnp.concatenate([b_group, b_expert, jnp.zeros((pad,), jnp.float32)])[None, :]
    return wr, br


def _mixer_ab_layer(x, w_in, gn_g, pool_w, pool_scale, w_out, ln_g, ln_b, wr, br):
    B, T, D = x.shape
    TM = TOK_TM
    C = RET_CHUNK
    H = RET_HEADS
    bf = jnp.bfloat16
    log_g = jnp.log1p(-jnp.exp2(-5.0 - jnp.arange(H, dtype=jnp.float32)))
    i = jnp.arange(C, dtype=jnp.float32)
    diff = i[:, None] - i[None, :]
    idec = jnp.where(diff >= 0, jnp.exp(log_g[:, None, None] * jnp.maximum(diff, 0.0)), 0.0)
    qdec = jnp.broadcast_to(jnp.exp(log_g[:, None] * (i + 1.0))[:, :, None], (H, C, RET_DK))
    kdec = jnp.broadcast_to(jnp.exp(log_g[:, None] * (C - 1.0 - i))[:, :, None], (H, C, RET_DK))
    cdec = jnp.exp(log_g * C)
    r = jnp.arange(TM)[:, None]
    bandc = jnp.stack([((r - jnp.arange(TM)[None, :] >= 0) & (r - jnp.arange(TM)[None, :] < w))
                       for w in POOL_WINDOWS]).astype(bf)
    jp = jnp.arange(POOL_TAIL)[None, :] - POOL_TAIL
    bandp = jnp.stack([((r - jp) < w) for w in POOL_WINDOWS]).astype(bf)
    nG = len(POOL_WINDOWS)
    return pl.pallas_call(
        _mixer_ab_kernel,
        grid=(B, T // TM),
        in_specs=[pl.BlockSpec(memory_space=pltpu.SMEM),
                  pl.BlockSpec((1, TM, D), lambda b, t: (b, t, 0)),
                  _const_spec(w_in.shape), _const_spec((H, C, C)), _const_spec((H, C, RET_DK)),
                  _const_spec((H, C, RET_DK)), _const_spec((1, H * RET_DV)),
                  _const_spec((nG, TM, TM)), _const_spec((nG, TM, POOL_TAIL)),
                  _const_spec((nG, POOL_GROUP, POOL_GROUP)), _const_spec((1, nG * POOL_GROUP)),
                  _const_spec(w_out.shape), _const_spec((1, D)), _const_spec((1, D)),
                  _const_spec((D, ROUTE_W)), _const_spec((1, ROUTE_W)), _const_spec((TM, TM))],
        out_specs=[pl.BlockSpec((1, TM, D), lambda b, t: (b, t, 0)),
                   pl.BlockSpec((1, TM, ROUTE_W), lambda b, t: (b, t, 0)),
                   _const_spec((8, ROUTE_W))],
        out_shape=[jax.ShapeDtypeStruct((B, T, D), jnp.float32),
                   jax.ShapeDtypeStruct((B, T, ROUTE_W), jnp.float32),
                   jax.ShapeDtypeStruct((8, ROUTE_W), jnp.float32)],
        scratch_shapes=[pltpu.VMEM((H, RET_DK, RET_DV), jnp.float32),
                        pltpu.VMEM((POOL_TAIL, nG * POOL_GROUP), jnp.float32),
                        pltpu.VMEM((TM, w_out.shape[0]), bf),
                        pltpu.VMEM((8, ROUTE_W), jnp.float32)],
        compiler_params=pltpu.CompilerParams(dimension_semantics=("arbitrary", "arbitrary"),
                                             vmem_limit_bytes=VMEM_LIMIT),
        name="mixer_ab_layer",
    )(cdec, x, w_in.astype(bf), idec, qdec, kdec, gn_g[None, :], bandc, bandp,
      pool_w.astype(bf), pool_scale[None, :], w_out.astype(bf), ln_g[None, :], ln_b[None, :], wr, br,
      _strict_lower_tri(TM))


def _outproj_kernel(o_ref, x_ref, wout_ref, lng_ref, lnb_ref, wr_ref, br_ref, tri_ref,
                    x1_ref, route_ref, cnt_ref, run_scr):
    @pl.when(pl.program_id(0) == 0)
    def _():
        run_scr[...] = jnp.zeros(run_scr.shape, jnp.float32)

    h = _bdot(o_ref[...], wout_ref[...])
    x1 = _ln_rows(ALPHA * x_ref[...] + h, lng_ref[...], lnb_ref[...])
    x1_ref[...] = x1
    route_ref[...] = _route_rows(x1, wr_ref, br_ref, tri_ref, run_scr, cnt_ref)


def _strict_lower_tri(n):
    r = jnp.arange(n)
    return (r[:, None] > r[None, :]).astype(jnp.bfloat16)


def _outproj_layer(o, x, w_out, ln_g, ln_b, wr, br):
    N, D = x.shape
    TM = TOK_TM
    cs = lambda shape: pl.BlockSpec(shape, lambda i: (0,) * len(shape))
    return pl.pallas_call(
        _outproj_kernel,
        grid=(N // TM,),
        in_specs=[pl.BlockSpec((TM, o.shape[1]), lambda i: (i, 0)),
                  pl.BlockSpec((TM, D), lambda i: (i, 0)),
                  cs(w_out.shape), cs((1, D)), cs((1, D)), cs((D, ROUTE_W)), cs((1, ROUTE_W)),
                  cs((TM, TM))],
        out_specs=[pl.BlockSpec((TM, D), lambda i: (i, 0)),
                   pl.BlockSpec((TM, ROUTE_W), lambda i: (i, 0)),
                   cs((8, ROUTE_W))],
        out_shape=[jax.ShapeDtypeStruct((N, D), jnp.float32),
                   jax.ShapeDtypeStruct((N, ROUTE_W), jnp.float32),
                   jax.ShapeDtypeStruct((8, ROUTE_W), jnp.float32)],
        scratch_shapes=[pltpu.VMEM((8, ROUTE_W), jnp.float32)],
        compiler_params=pltpu.CompilerParams(dimension_semantics=("arbitrary",),
                                             vmem_limit_bytes=VMEM_LIMIT),
        name="outproj_layer",
    )(o, x, w_out.astype(jnp.bfloat16), ln_g[None, :], ln_b[None, :], wr, br, _strict_lower_tri(TM))


LANES = 128
QB = Q_BLOCK
DSA_TK = 512
DSA_HG = 2
DSA_VALUE_STEPS = 24
DSA_ROUNDS_PER_TEST = 4
NEG_BIG = -1e30
KEY_LO0 = -2139095040
KEY_HI0 = 2139095040
LOG2E = 1.4426950408889634
BIG_IDX = 1 << 30


def _rep(x, n, axis):
    return jnp.concatenate([x] * n, axis=axis) if n > 1 else x


def _sum_lane_chunks(v, width):
    out = v[:, 0:LANES]
    for c in range(1, width // LANES):
        out = out + v[:, c * LANES:(c + 1) * LANES]
    return out


def _dsa_kernel(slopes_ref, cq_ref, qidx_ref, widx_ref, kidxT_ref, ckvT_ref, ckv_ref,
                wuq_ref, wukT_ref, wuvp_ref, o_ref,
                keys_scr, wb_scr, q_scr, a_scr, m_scr, l_scr, acc_scr,
                th_scr, c_scr, *, topk, seq_len):
    f32 = jnp.float32
    TK = DSA_TK
    NCH = TK // LANES
    H = C_HEADS
    qb = pl.program_id(1)
    t0 = qb * QB
    n_kt = (t0 + QB - 1) // TK + 1
    row = lax.broadcasted_iota(jnp.int32, (QB, TK), 0)
    col = lax.broadcasted_iota(jnp.int32, (QB, TK), 1)
    kf = float(topk)

    for h in range(IDX_HEADS):
        wb_scr[h] = jnp.broadcast_to(widx_ref[0, :, h:h + 1], (QB, LANES))

    def idx_body(j, carry):
        kT = kidxT_ref[0, j]
        acc = jnp.zeros((QB, TK), f32)
        for h in range(IDX_HEADS):
            s = jnp.dot(qidx_ref[0, h], kT, preferred_element_type=f32)
            acc = acc + _rep(wb_scr[h], NCH, axis=1) * jnp.maximum(s, 0.0)
        causal = (j * TK + col) <= (t0 + row)
        sc = jnp.where(causal, acc, -jnp.inf)
        b = pltpu.bitcast(sc, jnp.int32)
        keys_scr[j] = b ^ ((b >> 31) & 0x7FFFFFFF)
        vmax, vmin = carry
        sc_hi = jnp.where(causal, acc, jnp.inf)
        for c in range(NCH):
            vmax = jnp.maximum(vmax, sc[:, c * LANES:(c + 1) * LANES])
            vmin = jnp.minimum(vmin, sc_hi[:, c * LANES:(c + 1) * LANES])
        return vmax, vmin

    vmax, vmin = lax.fori_loop(0, n_kt, idx_body, (jnp.full((QB, LANES), -jnp.inf, f32),
                                                   jnp.full((QB, LANES), jnp.inf, f32)))

    def to_key(v):
        b = pltpu.bitcast(v, jnp.int32)
        return b ^ ((b >> 31) & 0x7FFFFFFF)

    def from_key(k):
        return pltpu.bitcast(k ^ ((k >> 31) & 0x7FFFFFFF), f32)

    def count_where(ones_fn):
        def body(j, c):
            return c + _sum_lane_chunks(ones_fn(j, keys_scr[j]), TK)
        c = lax.fori_loop(0, n_kt, body, jnp.zeros((QB, LANES), f32))
        return jnp.broadcast_to(jnp.sum(c, axis=1, keepdims=True), (QB, LANES))

    n_valid = (t0 + 1 + lax.broadcasted_iota(jnp.int32, (QB, LANES), 0)).astype(f32)
    small = n_valid <= kf
    kmax = to_key(jnp.broadcast_to(jnp.max(vmax, axis=1, keepdims=True), (QB, LANES)))
    kmin = to_key(jnp.broadcast_to(jnp.min(vmin, axis=1, keepdims=True), (QB, LANES)))
    lo0 = jnp.where(small, KEY_LO0, kmin)
    hi0 = jnp.where(small, KEY_LO0 + 1, kmax + 1)

    def row_done(lo, hi, clo):
        return jnp.where(clo == kf, 1.0, jnp.where(hi == lo + 1, 1.0, 0.0))

    def search_cond(st):
        it, pending = st[0], st[1]
        return (pending > 0) & (it < DSA_VALUE_STEPS + 32 + DSA_ROUNDS_PER_TEST)

    def search_round(it, lo, hi, clo, chi):
        mid_v = to_key(0.5 * from_key(lo) + 0.5 * from_key(hi - 1))
        mid_k = (lo & hi) + ((lo ^ hi) >> 1)
        mid = jnp.where(it < DSA_VALUE_STEPS, mid_v, mid_k)
        mid = jnp.minimum(jnp.maximum(mid, lo + 1), hi - 1)
        midb = _rep(mid, NCH, axis=1)
        cnt = count_where(lambda j, k: jnp.where(k >= midb, 1.0, 0.0))
        done = row_done(lo, hi, clo) > 0.0
        ge = cnt >= kf
        return (jnp.where(done, lo, jnp.where(ge, mid, lo)), jnp.where(done, hi, jnp.where(ge, hi, mid)),
                jnp.where(done, clo, jnp.where(ge, cnt, clo)), jnp.where(done, chi, jnp.where(ge, chi, cnt)))

    def search_body(st):
        it, _, lo, hi, clo, chi = st
        for r in range(DSA_ROUNDS_PER_TEST):
            lo, hi, clo, chi = search_round(it + r, lo, hi, clo, chi)
        pending = (jnp.min(row_done(lo, hi, clo)) < 0.5).astype(jnp.int32)
        return it + DSA_ROUNDS_PER_TEST, pending, lo, hi, clo, chi

    pending0 = (jnp.min(row_done(lo0, hi0, n_valid)) < 0.5).astype(jnp.int32)
    _, _, theta, _, n_ge, n_gt = lax.while_loop(
        search_cond, search_body,
        (jnp.int32(0), pending0, lo0, hi0, n_valid, jnp.zeros((QB, LANES), f32)))
    th_scr[...] = theta
    thb = _rep(theta, NCH, axis=1)

    need = kf - n_gt
    excess = n_ge > kf
    c_scr[...] = jnp.full((QB, LANES), BIG_IDX, jnp.int32)

    @pl.when(jnp.max(jnp.where(excess, 1.0, 0.0)) > 0.0)
    def _():
        def tie_body(i, carry):
            lo, hi = carry
            mid = (lo + hi) >> 1
            midb = _rep(mid, NCH, axis=1)
            cnt = count_where(lambda j, k: jnp.where(
                k == thb, jnp.where((j * TK + col) <= midb, 1.0, 0.0), 0.0))
            ok = cnt >= need
            return jnp.where(ok, lo, mid + 1), jnp.where(ok, mid, hi)
        nbits = max(1, (seq_len - 1).bit_length())
        lo, _ = lax.fori_loop(0, nbits, tie_body,
                              (jnp.zeros((QB, LANES), jnp.int32),
                               jnp.full((QB, LANES), seq_len - 1, jnp.int32)))
        c_scr[...] = jnp.where(excess, lo, BIG_IDX)

    cq = cq_ref[0]
    for h in range(H):
        qh = jnp.dot(cq, wuq_ref[h], preferred_element_type=f32).astype(jnp.bfloat16)
        qa = jnp.dot(qh, wukT_ref[h], preferred_element_type=f32) * (C_HEAD_DIM ** -0.5 * LOG2E)
        q_scr[h * QB:(h + 1) * QB, :] = qa.astype(jnp.bfloat16)

    m_scr[...] = jnp.full(m_scr.shape, NEG_BIG, f32)
    l_scr[...] = jnp.zeros(l_scr.shape, f32)
    acc_scr[...] = jnp.zeros(acc_scr.shape, f32)

    def att_body(j, carry):
        kT = ckvT_ref[0, j]
        kv = ckv_ref[0, j]
        k = keys_scr[j]
        thb_ = _rep(th_scr[...], NCH, axis=1)
        cb_ = _rep(c_scr[...], NCH, axis=1)
        spos = j * TK + col
        keep_tie = jnp.where(spos <= cb_, 0.0, NEG_BIG)
        a_scr[...] = jnp.where(k > thb_, 0.0, jnp.where(k == thb_, keep_tie, NEG_BIG))
        srel8 = (j * TK - t0 + lax.broadcasted_iota(jnp.int32, (8, TK), 1)).astype(f32)

        for g in range(H // DSA_HG):
            g0 = g * DSA_HG * QB
            lg = jnp.dot(q_scr[g0:g0 + DSA_HG * QB, :], kT, preferred_element_type=f32)
            ps, alphas = [], []
            for u in range(DSA_HG):
                h = g * DSA_HG + u
                r = slice(h * QB, (h + 1) * QB)
                bh = _rep(slopes_ref[h] * srel8, QB // 8, axis=0)
                l = lg[u * QB:(u + 1) * QB, :] + a_scr[...] + bh
                m_old = m_scr[r, :]
                m_new = jnp.maximum(m_old, jnp.max(l, axis=1, keepdims=True))
                p = jnp.exp2(l - _rep(m_new, NCH, axis=1))
                alpha = jnp.exp2(m_old - m_new)
                rs = jnp.sum(_sum_lane_chunks(p, TK), axis=1, keepdims=True)
                l_scr[r, :] = alpha * l_scr[r, :] + rs
                m_scr[r, :] = m_new
                ps.append(p.astype(jnp.bfloat16))
                alphas.append(_rep(alpha, C_KV_RANK // LANES, axis=1))
            pv = jnp.dot(jnp.concatenate(ps, axis=0), kv, preferred_element_type=f32)
            acc_scr[g0:g0 + DSA_HG * QB, :] = (acc_scr[g0:g0 + DSA_HG * QB, :]
                                               * jnp.concatenate(alphas, axis=0) + pv)
        return carry

    lax.fori_loop(0, n_kt, att_body, 0)

    for g in range(H // 2):
        parts = []
        for hh in (2 * g, 2 * g + 1):
            inv = 1.0 / l_scr[hh * QB:(hh + 1) * QB, :]
            ol = acc_scr[hh * QB:(hh + 1) * QB, :] * _rep(inv, C_KV_RANK // LANES, axis=1)
            parts.append(jnp.dot(ol.astype(jnp.bfloat16), wuvp_ref[hh], preferred_element_type=f32))
        o_ref[0, :, g * LANES:(g + 1) * LANES] = parts[0] + parts[1]


def _dsa_proj_kernel(x_ref, wq_ref, wkv_ref, wkvT_ref, wqi_ref, wkiT_ref, wwi_ref, qn_ref, kvn_ref, kvnT_ref,
                     cq_ref, ckv_ref, ckvT_ref, qidx_ref, kidxT_ref, widx_ref):
    f32 = jnp.float32
    bf = jnp.bfloat16
    xb = x_ref[...].astype(bf)

    def rms(z, g):
        return z * lax.rsqrt(jnp.mean(z * z, axis=1, keepdims=True) + 1e-6) * g

    cq_ref[...] = rms(jnp.dot(xb, wq_ref[...], preferred_element_type=f32), qn_ref[...]).astype(bf)
    ckv = rms(jnp.dot(xb, wkv_ref[...], preferred_element_type=f32), kvn_ref[...])
    ckv_ref[0] = ckv.astype(bf)
    nt = (((1,), (1,)), ((), ()))
    zT = lax.dot_general(wkvT_ref[...], xb, nt, preferred_element_type=f32)
    ssq = jnp.sum(zT * zT, axis=0, keepdims=True) * (1.0 / C_KV_RANK)
    ckvT_ref[0] = (zT * lax.rsqrt(ssq + 1e-6) * kvnT_ref[...]).astype(bf)
    zqi = jnp.dot(xb, wqi_ref[...], preferred_element_type=f32)
    for h in range(IDX_HEADS):
        qidx_ref[0, h] = zqi[:, h * IDX_DIM:(h + 1) * IDX_DIM].astype(bf)
    kidxT_ref[0] = lax.dot_general(wkiT_ref[...], xb, nt, preferred_element_type=f32).astype(bf)
    widx_ref[...] = jnp.dot(xb, wwi_ref[...], preferred_element_type=f32)[:, 0:IDX_HEADS]


def _dsa_proj(x, w_in, q_norm, kv_norm):
    B, T, D = x.shape
    N = B * T
    TK = DSA_TK
    NT = T // TK
    bf = jnp.bfloat16
    s1 = C_Q_RANK
    s2 = s1 + C_KV_RANK
    s3 = s2 + IDX_HEADS * IDX_DIM
    s4 = s3 + IDX_DIM
    wq = w_in[:, :s1].astype(bf)
    wkv = w_in[:, s1:s2].astype(bf)
    wqi = w_in[:, s2:s3].astype(bf)
    wkvT = wkv.T
    wkiT = w_in[:, s3:s4].T.astype(bf)
    wwi = jnp.pad(w_in[:, s4:], ((0, 0), (0, LANES - IDX_HEADS))).astype(bf)
    cs = lambda shape: pl.BlockSpec(shape, lambda i: (0,) * len(shape))
    cq, ckv, ckvT, qidx, kidxT, widx = pl.pallas_call(
        _dsa_proj_kernel,
        grid=(N // TK,),
        in_specs=[pl.BlockSpec((TK, D), lambda i: (i, 0)),
                  cs(wq.shape), cs(wkv.shape), cs(wkvT.shape), cs(wqi.shape), cs(wkiT.shape), cs(wwi.shape),
                  cs((1, C_Q_RANK)), cs((1, C_KV_RANK)), cs((C_KV_RANK, 1))],
        out_specs=[pl.BlockSpec((TK, C_Q_RANK), lambda i: (i, 0)),
                   pl.BlockSpec((1, TK, C_KV_RANK), lambda i: (i, 0, 0)),
                   pl.BlockSpec((1, C_KV_RANK, TK), lambda i: (i, 0, 0)),
                   pl.BlockSpec((1, IDX_HEADS, TK, IDX_DIM), lambda i: (i // NT, 0, i % NT, 0)),
                   pl.BlockSpec((1, IDX_DIM, TK), lambda i: (i, 0, 0)),
                   pl.BlockSpec((TK, IDX_HEADS), lambda i: (i, 0))],
        out_shape=[jax.ShapeDtypeStruct((N, C_Q_RANK), bf),
                   jax.ShapeDtypeStruct((B * NT, TK, C_KV_RANK), bf),
                   jax.ShapeDtypeStruct((B * NT, C_KV_RANK, TK), bf),
                   jax.ShapeDtypeStruct((B, IDX_HEADS, T, IDX_DIM), bf),
                   jax.ShapeDtypeStruct((B * NT, IDX_DIM, TK), bf),
                   jax.ShapeDtypeStruct((N, IDX_HEADS), jnp.float32)],
        compiler_params=pltpu.CompilerParams(dimension_semantics=("arbitrary",),
                                             vmem_limit_bytes=VMEM_LIMIT),
        name="dsa_proj",
    )(x.reshape(N, D), wq, wkv, wkvT, wqi, wkiT, wwi, q_norm[None, :], kv_norm[None, :], kv_norm[:, None])
    return (cq.reshape(B, T, C_Q_RANK), qidx, widx.reshape(B, T, IDX_HEADS),
            kidxT.reshape(B, NT, IDX_DIM, TK), ckvT.reshape(B, NT, C_KV_RANK, TK),
            ckv.reshape(B, NT, TK, C_KV_RANK))


def _dsa_attention(cq, qidx, w_idx, kidxT, ckvT, ckv, w_uq, w_uk, w_uv):
    B, T, _ = cq.shape
    topk = min(IDX_TOPK_MAX, T // 4)
    TK = DSA_TK
    NT = T // TK
    H = C_HEADS
    bf = jnp.bfloat16
    slopes2 = jnp.exp2(-8.0 * jnp.arange(1, H + 1, dtype=jnp.float32) / H) * LOG2E
    wuq =w_uq.reshape(C_Q_RANK, H, C_HEAD_DIM).transpose(1, 0, 2).astype(bf)
    wukT = w_uk.transpose(1, 2, 0).astype(bf)
    wuv = w_uv.transpose(1, 0, 2)
    zeros = jnp.zeros_like(wuv)
    even = (jnp.arange(H) % 2 == 0)[:, None, None]
    wuvp = jnp.where(even, jnp.concatenate([wuv, zeros], -1), jnp.concatenate([zeros, wuv], -1)).astype(bf)

    kern = functools.partial(_dsa_kernel, topk=topk, seq_len=T)
    full = lambda shape: pl.BlockSpec(shape, lambda b, q, *_: (0,) * len(shape))
    grid_spec = pltpu.PrefetchScalarGridSpec(
        num_scalar_prefetch=0,
        grid=(B, T // QB),
        in_specs=[
            pl.BlockSpec(memory_space=pltpu.SMEM),
            pl.BlockSpec((1, QB, C_Q_RANK), lambda b, q: (b, q, 0)),
            pl.BlockSpec((1, IDX_HEADS, QB, IDX_DIM), lambda b, q: (b, 0, q, 0)),
            pl.BlockSpec((1, QB, IDX_HEADS), lambda b, q: (b, q, 0)),
            pl.BlockSpec((1, NT, IDX_DIM, TK), lambda b, q: (b, 0, 0, 0)),
            pl.BlockSpec((1, NT, C_KV_RANK, TK), lambda b, q: (b, 0, 0, 0)),
            pl.BlockSpec((1, NT, TK, C_KV_RANK), lambda b, q: (b, 0, 0, 0)),
            full((H, C_Q_RANK, C_HEAD_DIM)),
            full((H, C_HEAD_DIM, C_KV_RANK)),
            full((H, C_KV_RANK, LANES)),
        ],
        out_specs=pl.BlockSpec((1, QB, H * C_HEAD_DIM), lambda b, q: (b, q, 0)),
        scratch_shapes=[
            pltpu.VMEM((NT, QB, TK), jnp.int32),
            pltpu.VMEM((IDX_HEADS, QB, LANES), jnp.float32),
            pltpu.VMEM((H * QB, C_KV_RANK), bf),
            pltpu.VMEM((QB, TK), jnp.float32),
            pltpu.VMEM((H * QB, LANES), jnp.float32),
            pltpu.VMEM((H * QB, LANES), jnp.float32),
            pltpu.VMEM((H * QB, C_KV_RANK), jnp.float32),
            pltpu.VMEM((QB, LANES), jnp.int32),
            pltpu.VMEM((QB, LANES), jnp.int32),
        ],
    )
    return pl.pallas_call(
        kern,
        grid_spec=grid_spec,
        out_shape=jax.ShapeDtypeStruct((B, T, H * C_HEAD_DIM), jnp.float32),
        compiler_params=pltpu.CompilerParams(dimension_semantics=("arbitrary", "arbitrary"),
                                             vmem_limit_bytes=56 * 1024 * 1024),
        name="dsa_attention",
    )(slopes2, cq, qidx, w_idx, kidxT, ckvT, ckv, wuq, wukT, wuvp)


def _mixer_c(x, w_in, q_norm, kv_norm, w_uq, w_uk, w_uv):
    return _dsa_attention(*_dsa_proj(x, w_in, q_norm, kv_norm), w_uq, w_uk, w_uv)


def _moe_layer(x1, route, cnt, p, w_gu, w_down, ln_g, ln_b, w_gate, w_proj):
    N = x1.shape[0]
    M = N * TOPK_IN_GROUP
    flat_e = route[:, 0:TOPK_IN_GROUP].astype(jnp.int32).reshape(M)
    rank = route[:, 4:4 + TOPK_IN_GROUP].astype(jnp.int32).reshape(M)
    counts = cnt[0, N_GROUPS:N_GROUPS + N_EXPERTS].astype(jnp.int32)
    return _moe_experts(x1, flat_e, rank, counts, route, p, w_gu, w_down, ln_g, ln_b, w_gate, w_proj)


MOE_BLK = 256
MOE_TM = 256
MOE_DMA_UNROLL = 8


def _dispatch_kernel(dest_ref, x_ref, xs_init_ref, xs_ref, sem):
    del xs_init_ref
    base = pl.program_id(0) * (TOPK_IN_GROUP * MOE_TM)

    def row_copy(t, k):
        d = dest_ref[base + TOPK_IN_GROUP * t + k]
        return pltpu.make_async_copy(x_ref.at[pl.ds(t, 1), :], xs_ref.at[pl.ds(d, 1), :], sem)

    def issue(t, c):
        for k in range(TOPK_IN_GROUP):
            row_copy(t, k).start(priority=k)
        return c

    def drain(t, c):
        for k in range(TOPK_IN_GROUP):
            row_copy(t, k).wait()
        return c

    lax.fori_loop(0, MOE_TM, issue, 0, unroll=MOE_DMA_UNROLL)
    lax.fori_loop(0, MOE_TM, drain, 0, unroll=MOE_DMA_UNROLL)


def _expert_kernel(be_ref, nu_ref, xs_ref, wgu_ref, wd_ref, ys_ref, wgu_bf, wd_bf):
    b = pl.program_id(0)
    F = wd_bf.shape[0]

    @pl.when(b < nu_ref[0])
    def _():
        e = be_ref[b]
        prev = be_ref[jnp.maximum(b - 1, 0)]

        @pl.when((b == 0) | (e != prev))
        def _():
            wgu_bf[...] = wgu_ref[0].astype(jnp.bfloat16)
            wd_bf[...] = wd_ref[0].astype(jnp.bfloat16)

        h = jnp.dot(xs_ref[...].astype(jnp.bfloat16), wgu_bf[...], preferred_element_type=jnp.float32)
        a = h[:, :F]
        act = a * jax.nn.sigmoid(a) * h[:, F:]
        ys_ref[...] = jnp.dot(act.astype(jnp.bfloat16), wd_bf[...], preferred_element_type=jnp.float32)

    @pl.when(b >= nu_ref[0])
    def _():
        ys_ref[...] = jnp.zeros(ys_ref.shape, ys_ref.dtype)


def _combine_kernel(dest_ref, ys_ref, route_ref, x1_ref, p_ref, lng_ref, lnb_ref, wg_ref, wp_ref,
                    o_ref, buf, sem):
    i = pl.program_id(0)
    n = pl.num_programs(0)

    def row_copy(step, slot, t, k):
        d = dest_ref[step * (TOPK_IN_GROUP * MOE_TM) + TOPK_IN_GROUP * t + k]
        return pltpu.make_async_copy(ys_ref.at[pl.ds(d, 1), :], buf.at[slot, k, pl.ds(t, 1), :], sem.at[slot])

    def issue(step, slot):
        def body(t, c):
            for k in range(TOPK_IN_GROUP):
                row_copy(step, slot, t, k).start(priority=k)
            return c
        lax.fori_loop(0, MOE_TM, body, 0, unroll=MOE_DMA_UNROLL)

    def drain(step, slot):
        def body(t, c):
            for k in range(TOPK_IN_GROUP):
                row_copy(step, slot, t, k).wait()
            return c
        lax.fori_loop(0, MOE_TM, body, 0, unroll=MOE_DMA_UNROLL)

    @pl.when(i == 0)
    def _():
        issue(0, 0)

    @pl.when(i + 1 < n)
    def _():
        issue(i + 1, (i + 1) % 2)

    slot = i % 2
    drain(i, slot)
    g = route_ref[...]
    m = g[:, 2:3] * buf[slot, 0] + g[:, 3:4] * buf[slot, 1]
    x2 = _ln_rows(ALPHA * x1_ref[...] + m, lng_ref[...], lnb_ref[...])
    o_ref[...] = x2 + jax.nn.sigmoid(_bdot(x2, wg_ref[...])) * _bdot(p_ref[...], wp_ref[...])


def _moe_experts(xt, flat_e, rank, counts, route, p, w_gu, w_down, ln_g, ln_b, w_gate, w_proj):
    N, D = xt.shape
    M = flat_e.shape[0]
    F = w_down.shape[1]
    BLK, TM = MOE_BLK, MOE_TM
    nb = -(-(M + N_EXPERTS * (BLK - 1)) // BLK)
    R = nb * BLK
    padded = ((counts + BLK - 1) // BLK) * BLK
    pend = jnp.cumsum(padded)
    pstart = pend - padded
    dest = (pstart[flat_e] + rank).astype(jnp.int32)
    nused = (pend[-1:] // BLK).astype(jnp.int32)
    blk_e = jnp.minimum(jnp.sum(pend[None, :] <= (jnp.arange(nb) * BLK)[:, None], axis=1),
                        N_EXPERTS - 1).astype(jnp.int32)
    vmem = 56 * 1024 * 1024

    xs = pl.pallas_call(
        _dispatch_kernel,
        grid_spec=pltpu.PrefetchScalarGridSpec(
            num_scalar_prefetch=1,
            grid=(N // TM,),
            in_specs=[pl.BlockSpec((TM, D), lambda i, d: (i, 0)),
                      pl.BlockSpec(memory_space=pl.ANY)],
            out_specs=pl.BlockSpec(memory_space=pl.ANY),
            scratch_shapes=[pltpu.SemaphoreType.DMA(())],
        ),
        out_shape=jax.ShapeDtypeStruct((R, D), jnp.float32),
        input_output_aliases={2: 0},
        compiler_params=pltpu.CompilerParams(dimension_semantics=("arbitrary",), vmem_limit_bytes=vmem),
        name="moe_dispatch",
    )(dest, xt, jnp.zeros((R, D), jnp.float32))

    def blk(b, be, nu):
        return jnp.minimum(b, nu[0] - 1)

    ys = pl.pallas_call(
        _expert_kernel,
        grid_spec=pltpu.PrefetchScalarGridSpec(
            num_scalar_prefetch=2,
            grid=(nb,),
            in_specs=[pl.BlockSpec((BLK, D), lambda b, be, nu: (blk(b, be, nu), 0)),
                      pl.BlockSpec((1, D, 2 * F), lambda b, be, nu: (be[blk(b, be, nu)], 0, 0)),
                      pl.BlockSpec((1, F, D), lambda b, be, nu: (be[blk(b, be, nu)], 0, 0))],
            out_specs=pl.BlockSpec((BLK, D), lambda b, be, nu: (b, 0)),
            scratch_shapes=[pltpu.VMEM((D, 2 * F), jnp.bfloat16), pltpu.VMEM((F, D), jnp.bfloat16)],
        ),
        out_shape=jax.ShapeDtypeStruct((R, D), jnp.float32),
        compiler_params=pltpu.CompilerParams(dimension_semantics=("arbitrary",), vmem_limit_bytes=vmem),
        name="moe_experts",
    )(blk_e, nused, xs, w_gu, w_down)

    return pl.pallas_call(
        _combine_kernel,
        grid_spec=pltpu.PrefetchScalarGridSpec(
            num_scalar_prefetch=1,
            grid=(N // TM,),
            in_specs=[pl.BlockSpec(memory_space=pl.ANY),
                      pl.BlockSpec((TM, ROUTE_W), lambda i, d: (i, 0)),
                      pl.BlockSpec((TM, D), lambda i, d: (i, 0)),
                      pl.BlockSpec((TM, p.shape[1]), lambda i, d: (i, 0)),
                      pl.BlockSpec((1, D), lambda i, d: (0, 0)),
                      pl.BlockSpec((1, D), lambda i, d: (0, 0)),
                      pl.BlockSpec((D, D), lambda i, d: (0, 0)),
                      pl.BlockSpec((p.shape[1], D), lambda i, d: (0, 0))],
            out_specs=pl.BlockSpec((TM, D), lambda i, d: (i, 0)),
            scratch_shapes=[pltpu.VMEM((2, TOPK_IN_GROUP, TM, D), jnp.float32),
                            pltpu.SemaphoreType.DMA((2,))],
        ),
        out_shape=jax.ShapeDtypeStruct((N, D), jnp.float32),
        compiler_params=pltpu.CompilerParams(dimension_semantics=("arbitrary",), vmem_limit_bytes=vmem),
        name="moe_combine",
    )(dest, ys, route, xt, p, ln_g[None, :], ln_b[None, :],
      w_gate.astype(jnp.bfloat16), w_proj.astype(jnp.bfloat16))


def kernel(x, p, ab_w_in, ret_gn_g, pool_w, pool_scale, ab_w_out, c_w_in, c_q_norm, c_kv_norm, c_w_uq, c_w_uk, c_w_uv, c_w_out, ln1_g, ln1_b, ln2_g, ln2_b, moe_w_group, moe_b_group, moe_w_expert, moe_b_expert, moe_w_gu, moe_w_down, ple_w_gate, ple_w_proj):
    B, T, D = x.shape
    N = B * T
    for i in range(DEPTH):
        j = i // 2
        wr, br = _router_weights(moe_w_group[i], moe_b_group[i], moe_w_expert[i], moe_b_expert[i])
        if i % 2 == 0:
            x1, route, cnt = _mixer_ab_layer(x, ab_w_in[j], ret_gn_g[j], pool_w[j], pool_scale[j], ab_w_out[j],
                                             ln1_g[i], ln1_b[i], wr, br)
            x1, route = x1.reshape(N, D), route.reshape(N, ROUTE_W)
        else:
            o = _mixer_c(x, c_w_in[j], c_q_norm[j], c_kv_norm[j], c_w_uq[j], c_w_uk[j], c_w_uv[j])
            x1, route, cnt = _outproj_layer(o.reshape(N, -1), x.reshape(N, D), c_w_out[j],
                                            ln1_g[i], ln1_b[i], wr, br)
        x = _moe_layer(x1, route, cnt, p[i].reshape(N, -1), moe_w_gu[i], moe_w_down[i],
                       ln2_g[i], ln2_b[i], ple_w_gate[i], ple_w_proj[i]).reshape(B, T, D)
    return x
```

```python
import functools
import math
import jax
import jax.numpy as jnp
from jax import lax
from jax.experimental import pallas as pl
from jax.experimental.pallas import tpu as pltpu

D_MODEL = 1024
DEPTH = 2
RET_HEADS = 4
RET_DK = 128
RET_DV = 128
RET_CHUNK = 128
POOL_WINDOWS = (2, 4, 8, 16)
POOL_GROUP = 128
C_HEADS = 16
C_HEAD_DIM = 64
C_Q_RANK = 256
C_KV_RANK = 256
IDX_HEADS = 8
IDX_DIM = 64
IDX_TOPK_MAX = 256
Q_BLOCK = 128
N_GROUPS = 4
EXPERTS_PER_GROUP = 8
N_EXPERTS = N_GROUPS * EXPERTS_PER_GROUP
TOPK_IN_GROUP = 2
MOE_BLOCK = 128
ALPHA = (2.0 * DEPTH) ** 0.25
LN_EPS = 1e-5


def _mm_kernel(x_ref, w_ref, o_ref):
    o_ref[...] = jnp.dot(x_ref[...].astype(jnp.bfloat16), w_ref[...].astype(jnp.bfloat16),
                         preferred_element_type=jnp.float32)


def _matmul(x, w, tm=512):
    M, K = x.shape
    N = w.shape[1]
    npad = (-N) % 128
    if npad:
        w = jnp.pad(w, ((0, 0), (0, npad)))
    Np = N + npad
    out = pl.pallas_call(
        _mm_kernel,
        grid=(M // tm,),
        in_specs=[pl.BlockSpec((tm, K), lambda i: (i, 0)),
                  pl.BlockSpec((K, Np), lambda i: (0, 0))],
        out_specs=pl.BlockSpec((tm, Np), lambda i: (i, 0)),
        out_shape=jax.ShapeDtypeStruct((M, Np), jnp.float32),
        compiler_params=pltpu.CompilerParams(dimension_semantics=("arbitrary",),
                                             vmem_limit_bytes=56 * 1024 * 1024),
    )(x, w)
    return out[:, :N] if npad else out


def _mm3(x, w):
    B, T, K = x.shape
    return _matmul(x.reshape(B * T, K), w).reshape(B, T, w.shape[1])


def _layer_norm(x, g, b):
    mu = x.mean(-1, keepdims=True)
    var = jnp.square(x - mu).mean(-1, keepdims=True)
    return (x - mu) * lax.rsqrt(var + LN_EPS) * g + b


def _rms_norm(x, g):
    return x * lax.rsqrt(jnp.mean(x * x, -1, keepdims=True) + 1e-6) * g


TOK_TM = 512
POOL_TAIL = 16
VMEM_LIMIT = 56 * 1024 * 1024
ROUTE_W = 128
ROUTE_ROWS = 8


def _bdot(a, b):
    return jnp.dot(a.astype(jnp.bfloat16), b.astype(jnp.bfloat16), preferred_element_type=jnp.float32)


def _ln_rows(v, g, b):
    mu = jnp.mean(v, axis=1, keepdims=True)
    d = v - mu
    var = jnp.mean(d * d, axis=1, keepdims=True)
    return d * lax.rsqrt(var + LN_EPS) * g + b


def _route_rows(x1, wr_ref, br_ref, tri_ref, run_scr, cnt_ref):
    f32 = jnp.float32
    logits = _bdot(x1, wr_ref[...]) + br_ref[...]
    lane = lax.broadcasted_iota(jnp.int32, logits.shape, 1)
    lane_f = lane.astype(f32)
    ninf = -jnp.inf
    gl = jnp.where(lane < N_GROUPS, logits, ninf)
    gmax = jnp.max(gl, axis=1, keepdims=True)
    grp = jnp.min(jnp.where(gl == gmax, lane_f, float(ROUTE_W)), axis=1, keepdims=True)
    g_w = 1.0 / jnp.sum(jnp.exp(gl - gmax), axis=1, keepdims=True)
    egrp = ((lane - N_GROUPS) >> 3).astype(f32)
    el = jnp.where(lane >= N_GROUPS, jnp.where(egrp == grp, logits, ninf), ninf)
    v1 = jnp.max(el, axis=1, keepdims=True)
    i1 = jnp.min(jnp.where(el == v1, lane_f, float(ROUTE_W)), axis=1, keepdims=True)
    el2 = jnp.where(lane_f == i1, ninf, el)
    v2 = jnp.max(el2, axis=1, keepdims=True)
    i2 = jnp.min(jnp.where(el2 == v2, lane_f, float(ROUTE_W)), axis=1, keepdims=True)
    e = jnp.exp(v2 - v1)
    den = 1.0 / (1.0 + e)
    oh1 = jnp.where(lane_f == i1, 1.0, 0.0)
    oh2 = jnp.where(lane_f == i2, 1.0, 0.0)
    oh = oh1 + oh2
    base = run_scr[0:1, :] + jnp.dot(tri_ref[...], oh.astype(jnp.bfloat16), preferred_element_type=f32)
    r1 = jnp.sum(oh1 * base, axis=1, keepdims=True)
    r2 = jnp.sum(oh2 * base, axis=1, keepdims=True)
    run_new = run_scr[0:1, :] + jnp.sum(oh, axis=0, keepdims=True)
    run_scr[0:1, :] = run_new
    cnt_ref[...] = jnp.broadcast_to(run_new, cnt_ref.shape)
    sel = lambda l, v, rest: jnp.where(lane == l, v, rest)
    return sel(0, i1 - N_GROUPS, sel(1, i2 - N_GROUPS, sel(2, g_w * den, sel(3, g_w * e * den,
                                                                             sel(4, r1, sel(5, r2, 0.0))))))


def _mixer_ab_kernel(cdec_ref, x_ref, win_ref, idec_ref, qdec_ref, kdec_ref, gng_ref, bandc_ref, bandp_ref,
                     pw_ref, psc_ref, wout_ref, lng_ref, lnb_ref, wr_ref, br_ref, tri_ref,
                     x1_ref, route_ref, cnt_ref, routeT_ref, r_scr, tail_scr, mixed_scr, run_scr):
    f32 = jnp.float32
    bf = jnp.bfloat16
    C = RET_CHUNK
    TM = x_ref.shape[1]
    nq = RET_HEADS * RET_DK
    nv = RET_HEADS * RET_DV
    ti = pl.program_id(1)

    @pl.when(ti == 0)
    def _():
        r_scr[...] = jnp.zeros(r_scr.shape, f32)
        tail_scr[...] = jnp.zeros(tail_scr.shape, f32)

    x = x_ref[0]
    xb = x.astype(bf)
    zq = jnp.dot(xb, win_ref[:, 0:nq], preferred_element_type=f32)
    zk = jnp.dot(xb, win_ref[:, nq:2 * nq], preferred_element_type=f32) * (RET_DK ** -0.5)
    zv = jnp.dot(xb, win_ref[:, 2 * nq:2 * nq + nv], preferred_element_type=f32)
    zg = jnp.dot(xb, win_ref[:, 2 * nq + nv:2 * nq + 2 * nv], preferred_element_type=f32)
    zu = jnp.dot(xb, win_ref[:, 2 * nq + 2 * nv:], preferred_element_type=f32)

    for h in range(RET_HEADS):
        hs = slice(h * RET_DK, (h + 1) * RET_DK)
        R = r_scr[h]
        for c in range(TM // C):
            cs = slice(c * C, (c + 1) * C)
            qc, kc, vc = zq[cs, hs], zk[cs, hs], zv[cs, hs]
            s = lax.dot_general(qc.astype(bf), kc.astype(bf), (((1,), (1,)), ((), ())),
                                preferred_element_type=f32) * idec_ref[h]
            o = _bdot(s, vc) + _bdot(qc * qdec_ref[h], R)
            R = R * cdec_ref[h] + lax.dot_general((kc * kdec_ref[h]).astype(bf), vc.astype(bf),
                                                  (((0,), (0,)), ((), ())), preferred_element_type=f32)
            mu = jnp.mean(o, axis=1, keepdims=True)
            d = o - mu
            var = jnp.mean(d * d, axis=1, keepdims=True)
            gate = zg[cs, hs]
            ret = d * lax.rsqrt(var + LN_EPS) * gng_ref[:, hs] * (gate * jax.nn.sigmoid(gate))
            mixed_scr[cs, hs] = ret.astype(bf)
        r_scr[h] = R

    t_glob = ti * TM + lax.broadcasted_iota(jnp.int32, (TM, POOL_GROUP), 0)
    tail = tail_scr[...]
    for gi, w in enumerate(POOL_WINDOWS):
        gs = slice(gi * POOL_GROUP, (gi + 1) * POOL_GROUP)
        u = zu[:, gs]
        u_hi = u.astype(bf)
        u_lo = (u - u_hi.astype(f32)).astype(bf)
        tl = tail[:, gs]
        t_hi = tl.astype(bf)
        t_lo = (tl - t_hi.astype(f32)).astype(bf)
        wsum = (jnp.dot(bandc_ref[gi], u_hi, preferred_element_type=f32)
                + jnp.dot(bandc_ref[gi], u_lo, preferred_element_type=f32)
                + jnp.dot(bandp_ref[gi], t_hi, preferred_element_type=f32)
                + jnp.dot(bandp_ref[gi], t_lo, preferred_element_type=f32))
        cnt = jnp.minimum(t_glob + 1, w).astype(f32)
        dpool = wsum / cnt - u
        y = _bdot(dpool, pw_ref[gi]) * psc_ref[:, gs]
        mixed_scr[:, nv + gi * POOL_GROUP:nv + (gi + 1) * POOL_GROUP] = y.astype(bf)
    tail_scr[...] = zu[TM - POOL_TAIL:, :]

    hmix = jnp.dot(mixed_scr[...], wout_ref[...], preferred_element_type=f32)
    x1 = _ln_rows(ALPHA * x + hmix, lng_ref[...], lnb_ref[...])
    x1_ref[0] = x1
    @pl.when((pl.program_id(0) == 0) & (ti == 0))
    def _():
        run_scr[...] = jnp.zeros(run_scr.shape, f32)

    route = _route_rows(x1, wr_ref, br_ref, tri_ref, run_scr, cnt_ref)
    route_ref[0] = route
    routeT_ref[...] = route.T[0:ROUTE_ROWS, :]


def _const_spec(shape):
    return pl.BlockSpec(shape, lambda b, t: (0,) * len(shape))


def _router_weights(w_group, b_group, w_expert, b_expert):
    D = w_group.shape[0]
    pad = ROUTE_W - N_GROUPS - N_EXPERTS
    wr = jnp.concatenate([w_group, w_expert, jnp.zeros((D, pad), jnp.float32)], axis=1).astype(jnp.bfloat16)
    br = jnp.concatenate([b_group, b_expert, jnp.zeros((pad,), jnp.float32)])[None, :]
    return wr, br


def _mixer_ab_layer(x, w_in, gn_g, pool_w, pool_scale, w_out, ln_g, ln_b, wr, br):
    B, T, D = x.shape
    TM = TOK_TM
    C = RET_CHUNK
    H = RET_HEADS
    bf = jnp.bfloat16
    log_g = jnp.log1p(-jnp.exp2(-5.0 - jnp.arange(H, dtype=jnp.float32)))
    i = jnp.arange(C, dtype=jnp.float32)
    diff = i[:, None] - i[None, :]
    idec = jnp.where(diff >= 0, jnp.exp(log_g[:, None, None] * jnp.maximum(diff, 0.0)), 0.0)
    qdec = jnp.broadcast_to(jnp.exp(log_g[:, None] * (i + 1.0))[:, :, None], (H, C, RET_DK))
    kdec = jnp.broadcast_to(jnp.exp(log_g[:, None] * (C - 1.0 - i))[:, :, None], (H, C, RET_DK))
    cdec = jnp.exp(log_g * C)
    r = jnp.arange(TM)[:, None]
    bandc = jnp.stack([((r - jnp.arange(TM)[None, :] >= 0) & (r - jnp.arange(TM)[None, :] < w))
                       for w in POOL_WINDOWS]).astype(bf)
    jp = jnp.arange(POOL_TAIL)[None, :] - POOL_TAIL
    bandp = jnp.stack([((r - jp) < w) for w in POOL_WINDOWS]).astype(bf)
    nG = len(POOL_WINDOWS)
    return pl.pallas_call(
        _mixer_ab_kernel,
        grid=(B, T // TM),
        in_specs=[pl.BlockSpec(memory_space=pltpu.SMEM),
                  pl.BlockSpec((1, TM, D), lambda b, t: (b, t, 0)),
                  _const_spec(w_in.shape), _const_spec((H, C, C)), _const_spec((H, C, RET_DK)),
                  _const_spec((H, C, RET_DK)), _const_spec((1, H * RET_DV)),
                  _const_spec((nG, TM, TM)), _const_spec((nG, TM, POOL_TAIL)),
                  _const_spec((nG, POOL_GROUP, POOL_GROUP)), _const_spec((1, nG * POOL_GROUP)),
                  _const_spec(w_out.shape), _const_spec((1, D)), _const_spec((1, D)),
                  _const_spec((D, ROUTE_W)), _const_spec((1, ROUTE_W)), _const_spec((TM, TM))],
        out_specs=[pl.BlockSpec((1, TM, D), lambda b, t: (b, t, 0)),
                   pl.BlockSpec((1, TM, ROUTE_W), lambda b, t: (b, t, 0)),
                   _const_spec((8, ROUTE_W)),
                   pl.BlockSpec((ROUTE_ROWS, TM), lambda b, t: (0, b * (T // TM) + t))],
        out_shape=[jax.ShapeDtypeStruct((B, T, D), jnp.float32),
                   jax.ShapeDtypeStruct((B, T, ROUTE_W), jnp.float32),
                   jax.ShapeDtypeStruct((8, ROUTE_W), jnp.float32),
                   jax.ShapeDtypeStruct((ROUTE_ROWS, B * T), jnp.float32)],
        scratch_shapes=[pltpu.VMEM((H, RET_DK, RET_DV), jnp.float32),
                        pltpu.VMEM((POOL_TAIL, nG * POOL_GROUP), jnp.float32),
                        pltpu.VMEM((TM, w_out.shape[0]), bf),
                        pltpu.VMEM((8, ROUTE_W), jnp.float32)],
        compiler_params=pltpu.CompilerParams(dimension_semantics=("arbitrary", "arbitrary"),
                                             vmem_limit_bytes=VMEM_LIMIT),
        name="mixer_ab_layer",
    )(cdec, x, w_in.astype(bf), idec, qdec, kdec, gn_g[None, :], bandc, bandp,
      pool_w.astype(bf), pool_scale[None, :], w_out.astype(bf), ln_g[None, :], ln_b[None, :], wr, br,
      _strict_lower_tri(TM))


def _outproj_kernel(o_ref, x_ref, wout_ref, lng_ref, lnb_ref, wr_ref, br_ref, tri_ref,
                    x1_ref, route_ref, cnt_ref, routeT_ref, run_scr):
    @pl.when(pl.program_id(0) == 0)
    def _():
        run_scr[...] = jnp.zeros(run_scr.shape, jnp.float32)

    h = _bdot(o_ref[...], wout_ref[...])
    x1 = _ln_rows(ALPHA * x_ref[...] + h, lng_ref[...], lnb_ref[...])
    x1_ref[...] = x1
    route = _route_rows(x1, wr_ref, br_ref, tri_ref, run_scr, cnt_ref)
    route_ref[...] = route
    routeT_ref[...] = route.T[0:ROUTE_ROWS, :]


def _strict_lower_tri(n):
    r = jnp.arange(n)
    return (r[:, None] > r[None, :]).astype(jnp.bfloat16)


def _outproj_layer(o, x, w_out, ln_g, ln_b, wr, br):
    N, D = x.shape
    TM = TOK_TM
    cs = lambda shape: pl.BlockSpec(shape, lambda i: (0,) * len(shape))
    return pl.pallas_call(
        _outproj_kernel,
        grid=(N // TM,),
        in_specs=[pl.BlockSpec((TM, o.shape[1]), lambda i: (i, 0)),
                  pl.BlockSpec((TM, D), lambda i: (i, 0)),
                  cs(w_out.shape), cs((1, D)), cs((1, D)), cs((D, ROUTE_W)), cs((1, ROUTE_W)),
                  cs((TM, TM))],
        out_specs=[pl.BlockSpec((TM, D), lambda i: (i, 0)),
                   pl.BlockSpec((TM, ROUTE_W), lambda i: (i, 0)),
                   cs((8, ROUTE_W)),
                   pl.BlockSpec((ROUTE_ROWS, TM), lambda i: (0, i))],
        out_shape=[jax.ShapeDtypeStruct((N, D), jnp.float32),
                   jax.ShapeDtypeStruct((N, ROUTE_W), jnp.float32),
                   jax.ShapeDtypeStruct((8, ROUTE_W), jnp.float32),
                   jax.ShapeDtypeStruct((ROUTE_ROWS, N), jnp.float32)],
        scratch_shapes=[pltpu.VMEM((8, ROUTE_W), jnp.float32)],
        compiler_params=pltpu.CompilerParams(dimension_semantics=("arbitrary",),
                                             vmem_limit_bytes=VMEM_LIMIT),
        name="outproj_layer",
    )(o, x, w_out.astype(jnp.bfloat16), ln_g[None, :], ln_b[None, :], wr, br, _strict_lower_tri(TM))


LANES = 128
QB = Q_BLOCK
DSA_TK = 512
DSA_HG = 4
DSA_VALUE_STEPS = 24
DSA_ROUNDS_PER_TEST = 4
NEG_BIG = -1e30
KEY_LO0 = -2139095040
KEY_HI0 = 2139095040
LOG2E = 1.4426950408889634
BIG_IDX = 1 << 30


def _rep(x, n, axis):
    return jnp.concatenate([x] * n, axis=axis) if n > 1 else x


def _sum_lane_chunks(v, width):
    out = v[:, 0:LANES]
    for c in range(1, width // LANES):
        out = out + v[:, c * LANES:(c + 1) * LANES]
    return out


def _dsa_kernel(slopes_ref, cq_ref, qidx_ref, widx_ref, kidxT_ref, ckvT_ref, ckv_ref,
                wuq_ref, wukT_ref, wuvp_ref, o_ref,
                keys_scr, wb_scr, q_scr, a_scr, m_scr, l_scr, acc_scr,
                th_scr, c_scr, *, topk, seq_len):
    f32 = jnp.float32
    TK = DSA_TK
    NCH = TK // LANES
    H = C_HEADS
    qb = pl.program_id(1)
    t0 = qb * QB
    n_kt = (t0 + QB - 1) // TK + 1
    row = lax.broadcasted_iota(jnp.int32, (QB, TK), 0)
    col = lax.broadcasted_iota(jnp.int32, (QB, TK), 1)
    kf = float(topk)

    for h in range(IDX_HEADS):
        wb_scr[h] = jnp.broadcast_to(widx_ref[0, :, h:h + 1], (QB, LANES))

    def idx_body(j, carry):
        kT = kidxT_ref[0, j]
        acc = jnp.zeros((QB, TK), f32)
        for h in range(IDX_HEADS):
            s = jnp.dot(qidx_ref[0, h], kT, preferred_element_type=f32)
            acc = acc + _rep(wb_scr[h], NCH, axis=1) * jnp.maximum(s, 0.0)
        causal = (j * TK + col) <= (t0 + row)
        sc = jnp.where(causal, acc, -jnp.inf)
        b = pltpu.bitcast(sc, jnp.int32)
        keys_scr[j] = b ^ ((b >> 31) & 0x7FFFFFFF)
        vmax, vmin = carry
        sc_hi = jnp.where(causal, acc, jnp.inf)
        for c in range(NCH):
            vmax = jnp.maximum(vmax, sc[:, c * LANES:(c + 1) * LANES])
            vmin = jnp.minimum(vmin, sc_hi[:, c * LANES:(c + 1) * LANES])
        return vmax, vmin

    vmax, vmin = lax.fori_loop(0, n_kt, idx_body, (jnp.full((QB, LANES), -jnp.inf, f32),
                                                   jnp.full((QB, LANES), jnp.inf, f32)))

    def to_key(v):
        b = pltpu.bitcast(v, jnp.int32)
        return b ^ ((b >> 31) & 0x7FFFFFFF)

    def from_key(k):
        return pltpu.bitcast(k ^ ((k >> 31) & 0x7FFFFFFF), f32)

    @pl.when(n_kt < keys_scr.shape[0])
    def _():
        keys_scr[n_kt] = jnp.full((QB, TK), KEY_LO0 - 1, jnp.int32)

    def count_where(ones_fn):
        def body(j2, c):
            j = 2 * j2
            return (c + _sum_lane_chunks(ones_fn(j, keys_scr[j]), TK)
                    + _sum_lane_chunks(ones_fn(j + 1, keys_scr[j + 1]), TK))
        c = lax.fori_loop(0, (n_kt + 1) // 2, body, jnp.zeros((QB, LANES), f32))
        return jnp.broadcast_to(jnp.sum(c, axis=1, keepdims=True), (QB, LANES))

    n_valid = (t0 + 1 + lax.broadcasted_iota(jnp.int32, (QB, LANES), 0)).astype(f32)
    small = n_valid <= kf
    kmax = to_key(jnp.broadcast_to(jnp.max(vmax, axis=1, keepdims=True), (QB, LANES)))
    kmin = to_key(jnp.broadcast_to(jnp.min(vmin, axis=1, keepdims=True), (QB, LANES)))
    lo0 = jnp.where(small, KEY_LO0, kmin)
    hi0 = jnp.where(small, KEY_LO0 + 1, kmax + 1)

    def row_done(lo, hi, clo):
        return jnp.where(clo == kf, 1.0, jnp.where(hi == lo + 1, 1.0, 0.0))

    def search_round(mid_fn, lo, hi, clo, chi):
        mid = jnp.minimum(jnp.maximum(mid_fn(lo, hi), lo + 1), hi - 1)
        midb = _rep(mid, NCH, axis=1)
        cnt = count_where(lambda j, k: jnp.where(k >= midb, 1.0, 0.0))
        ge = jnp.where(small, kf, cnt) >= kf
        return jnp.where(ge, mid, lo), jnp.where(ge, hi, mid), jnp.where(ge, cnt, clo), jnp.where(ge, chi, cnt)

    def search(mid_fn, max_rounds, st):
        def cond(s):
            return (s[1] > 0) & (s[0] < max_rounds)

        def body(s):
            it, _, lo, hi, clo, chi = s
            for _ in range(DSA_ROUNDS_PER_TEST):
                lo, hi, clo, chi = search_round(mid_fn, lo, hi, clo, chi)
            pending = (jnp.min(row_done(lo, hi, clo)) < 0.5).astype(jnp.int32)
            return it + DSA_ROUNDS_PER_TEST, pending, lo, hi, clo, chi

        return lax.while_loop(cond, body, st)

    pending0 = (jnp.min(row_done(lo0, hi0, n_valid)) < 0.5).astype(jnp.int32)
    st = (jnp.int32(0), pending0, lo0, hi0, n_valid, jnp.zeros((QB, LANES), f32))
    st = search(lambda lo, hi: to_key(0.5 * from_key(lo) + 0.5 * from_key(hi - 1)), DSA_VALUE_STEPS, st)
    st = search(lambda lo, hi: (lo & hi) + ((lo ^ hi) >> 1), 32 + DSA_ROUNDS_PER_TEST, (jnp.int32(0),) + st[1:])
    _, _, theta, _, n_ge, n_gt = st
    th_scr[...] = theta
    thb = _rep(theta, NCH, axis=1)

    need = kf - n_gt
    excess = n_ge > kf
    c_scr[...] = jnp.full((QB, LANES), BIG_IDX, jnp.int32)

    @pl.when(jnp.max(jnp.where(excess, 1.0, 0.0)) > 0.0)
    def _():
        def tie_body(i, carry):
            lo, hi = carry
            mid = (lo + hi) >> 1
            midb = _rep(mid, NCH, axis=1)
            cnt = count_where(lambda j, k: jnp.where(
                k == thb, jnp.where((j * TK + col) <= midb, 1.0, 0.0), 0.0))
            ok = cnt >= need
            return jnp.where(ok, lo, mid + 1), jnp.where(ok, mid, hi)
        nbits = max(1, (seq_len - 1).bit_length())
        lo, _ = lax.fori_loop(0, nbits, tie_body,
                              (jnp.zeros((QB, LANES), jnp.int32),
                               jnp.full((QB, LANES), seq_len - 1, jnp.int32)))
        c_scr[...] = jnp.where(excess, lo, BIG_IDX)

    cq = cq_ref[0]
    for h in range(H):
        qh = jnp.dot(cq, wuq_ref[h], preferred_element_type=f32).astype(jnp.bfloat16)
        qa = jnp.dot(qh, wukT_ref[h], preferred_element_type=f32) * (C_HEAD_DIM ** -0.5 * LOG2E)
        q_scr[h * QB:(h + 1) * QB, :] = qa.astype(jnp.bfloat16)

    m_scr[...] = jnp.full(m_scr.shape, NEG_BIG, f32)
    l_scr[...] = jnp.zeros(l_scr.shape, f32)
    acc_scr[...] = jnp.zeros(acc_scr.shape, f32)

    def att_body(j, carry):
        kT = ckvT_ref[0, j]
        kv = ckv_ref[0, j]
        k = keys_scr[j]
        thb_ = _rep(th_scr[...], NCH, axis=1)
        cb_ = _rep(c_scr[...], NCH, axis=1)
        spos = j * TK + col
        keep_tie = jnp.where(spos <= cb_, 0.0, NEG_BIG)
        a_scr[...] = jnp.where(k > thb_, 0.0, jnp.where(k == thb_, keep_tie, NEG_BIG))
        srel8 = (j * TK - t0 + lax.broadcasted_iota(jnp.int32, (8, TK), 1)).astype(f32)

        for g in range(H // DSA_HG):
            g0 = g * DSA_HG * QB
            lg = jnp.dot(q_scr[g0:g0 + DSA_HG * QB, :], kT, preferred_element_type=f32)
            ps, alphas = [], []
            for u in range(DSA_HG):
                h = g * DSA_HG + u
                r = slice(h * QB, (h + 1) * QB)
                bh = _rep(slopes_ref[h] * srel8, QB // 8, axis=0)
                l = lg[u * QB:(u + 1) * QB, :] + a_scr[...] + bh
                m_old = m_scr[r, :]
                m_new = jnp.maximum(m_old, jnp.max(l, axis=1, keepdims=True))
                p = jnp.exp2(l - _rep(m_new, NCH, axis=1))
                alpha = jnp.exp2(m_old - m_new)
                rs = jnp.sum(_sum_lane_chunks(p, TK), axis=1, keepdims=True)
                l_scr[r, :] = alpha * l_scr[r, :] + rs
                m_scr[r, :] = m_new
                ps.append(p.astype(jnp.bfloat16))
                alphas.append(_rep(alpha, C_KV_RANK // LANES, axis=1))
            pv = jnp.dot(jnp.concatenate(ps, axis=0), kv, preferred_element_type=f32)
            acc_scr[g0:g0 + DSA_HG * QB, :] = (acc_scr[g0:g0 + DSA_HG * QB, :]
                                               * jnp.concatenate(alphas, axis=0) + pv)
        return carry

    lax.fori_loop(0, n_kt, att_body, 0)

    for g in range(H // 2):
        parts = []
        for hh in (2 * g, 2 * g + 1):
            inv = 1.0 / l_scr[hh * QB:(hh + 1) * QB, :]
            ol = acc_scr[hh * QB:(hh + 1) * QB, :] * _rep(inv, C_KV_RANK // LANES, axis=1)
            parts.append(jnp.dot(ol.astype(jnp.bfloat16), wuvp_ref[hh], preferred_element_type=f32))
        o_ref[0, :, g * LANES:(g + 1) * LANES] = parts[0] + parts[1]


def _dsa_proj_kernel(x_ref, wq_ref, wkv_ref, wkvT_ref, wqi_ref, wkiT_ref, wwi_ref, qn_ref, kvn_ref, kvnT_ref,
                     cq_ref, ckv_ref, ckvT_ref, qidx_ref, kidxT_ref, widx_ref):
    f32 = jnp.float32
    bf = jnp.bfloat16
    xb = x_ref[...].astype(bf)

    def rms(z, g):
        return z * lax.rsqrt(jnp.mean(z * z, axis=1, keepdims=True) + 1e-6) * g

    cq_ref[...] = rms(jnp.dot(xb, wq_ref[...], preferred_element_type=f32), qn_ref[...]).astype(bf)
    ckv = rms(jnp.dot(xb, wkv_ref[...], preferred_element_type=f32), kvn_ref[...])
    ckv_ref[0] = ckv.astype(bf)
    nt = (((1,), (1,)), ((), ()))
    zT = lax.dot_general(wkvT_ref[...], xb, nt, preferred_element_type=f32)
    ssq = jnp.sum(zT * zT, axis=0, keepdims=True) * (1.0 / C_KV_RANK)
    ckvT_ref[0] = (zT * lax.rsqrt(ssq + 1e-6) * kvnT_ref[...]).astype(bf)
    zqi = jnp.dot(xb, wqi_ref[...], preferred_element_type=f32)
    for h in range(IDX_HEADS):
        qidx_ref[0, h] = zqi[:, h * IDX_DIM:(h + 1) * IDX_DIM].astype(bf)
    kidxT_ref[0] = lax.dot_general(wkiT_ref[...], xb, nt, preferred_element_type=f32).astype(bf)
    widx_ref[...] = jnp.dot(xb, wwi_ref[...], preferred_element_type=f32)[:, 0:IDX_HEADS]


def _dsa_proj(x, w_in, q_norm, kv_norm):
    B, T, D = x.shape
    N = B * T
    TK = DSA_TK
    NT = T // TK
    bf = jnp.bfloat16
    s1 = C_Q_RANK
    s2 = s1 + C_KV_RANK
    s3 = s2 + IDX_HEADS * IDX_DIM
    s4 = s3 + IDX_DIM
    wq = w_in[:, :s1].astype(bf)
    wkv = w_in[:, s1:s2].astype(bf)
    wqi = w_in[:, s2:s3].astype(bf)
    wkvT = wkv.T
    wkiT = w_in[:, s3:s4].T.astype(bf)
    wwi = jnp.pad(w_in[:, s4:], ((0, 0), (0, LANES - IDX_HEADS))).astype(bf)
    cs = lambda shape: pl.BlockSpec(shape, lambda i: (0,) * len(shape))
    cq, ckv, ckvT, qidx, kidxT, widx = pl.pallas_call(
        _dsa_proj_kernel,
        grid=(N // TK,),
        in_specs=[pl.BlockSpec((TK, D), lambda i: (i, 0)),
                  cs(wq.shape), cs(wkv.shape), cs(wkvT.shape), cs(wqi.shape), cs(wkiT.shape), cs(wwi.shape),
                  cs((1, C_Q_RANK)), cs((1, C_KV_RANK)), cs((C_KV_RANK, 1))],
        out_specs=[pl.BlockSpec((TK, C_Q_RANK), lambda i: (i, 0)),
                   pl.BlockSpec((1, TK, C_KV_RANK), lambda i: (i, 0, 0)),
                   pl.BlockSpec((1, C_KV_RANK, TK), lambda i: (i, 0, 0)),
                   pl.BlockSpec((1, IDX_HEADS, TK, IDX_DIM), lambda i: (i // NT, 0, i % NT, 0)),
                   pl.BlockSpec((1, IDX_DIM, TK), lambda i: (i, 0, 0)),
                   pl.BlockSpec((TK, IDX_HEADS), lambda i: (i, 0))],
        out_shape=[jax.ShapeDtypeStruct((N, C_Q_RANK), bf),
                   jax.ShapeDtypeStruct((B * NT, TK, C_KV_RANK), bf),
                   jax.ShapeDtypeStruct((B * NT, C_KV_RANK, TK), bf),
                   jax.ShapeDtypeStruct((B, IDX_HEADS, T, IDX_DIM), bf),
                   jax.ShapeDtypeStruct((B * NT, IDX_DIM, TK), bf),
                   jax.ShapeDtypeStruct((N, IDX_HEADS), jnp.float32)],
        compiler_params=pltpu.CompilerParams(dimension_semantics=("arbitrary",),
                                             vmem_limit_bytes=VMEM_LIMIT),
        name="dsa_proj",
    )(x.reshape(N, D), wq, wkv, wkvT, wqi, wkiT, wwi, q_norm[None, :], kv_norm[None, :], kv_norm[:, None])
    return (cq.reshape(B, T, C_Q_RANK), qidx, widx.reshape(B, T, IDX_HEADS),
            kidxT.reshape(B, NT, IDX_DIM, TK), ckvT.reshape(B, NT, C_KV_RANK, TK),
            ckv.reshape(B, NT, TK, C_KV_RANK))


def _dsa_attention(cq, qidx, w_idx, kidxT, ckvT, ckv, w_uq, w_uk, w_uv):
    B, T, _ = cq.shape
    topk = min(IDX_TOPK_MAX, T // 4)
    TK = DSA_TK
    NT = T // TK
    assert T % (2 * TK) == 0, "counting passes walk key tiles in pairs"
    H = C_HEADS
    bf = jnp.bfloat16
    slopes2 = jnp.exp2(-8.0 * jnp.arange(1, H + 1, dtype=jnp.float32) / H) * LOG2E
    wuq = w_uq.reshape(C_Q_RANK, H, C_HEAD_DIM).transpose(1, 0, 2).astype(bf)
    wukT = w_uk.transpose(1, 2, 0).astype(bf)
    wuv = w_uv.transpose(1, 0, 2)
    zeros = jnp.zeros_like(wuv)
    even = (jnp.arange(H) % 2 == 0)[:, None, None]
    wuvp = jnp.where(even, jnp.concatenate([wuv, zeros], -1), jnp.concatenate([zeros, wuv], -1)).astype(bf)

    kern = functools.partial(_dsa_kernel, topk=topk, seq_len=T)
    full = lambda shape: pl.BlockSpec(shape, lambda b, q, *_: (0,) * len(shape))
    grid_spec = pltpu.PrefetchScalarGridSpec(
        num_scalar_prefetch=0,
        grid=(B, T // QB),
        in_specs=[
            pl.BlockSpec(memory_space=pltpu.SMEM),
            pl.BlockSpec((1, QB, C_Q_RANK), lambda b, q: (b, q, 0)),
            pl.BlockSpec((1, IDX_HEADS, QB, IDX_DIM), lambda b, q: (b, 0, q, 0)),
            pl.BlockSpec((1, QB, IDX_HEADS), lambda b, q: (b, q, 0)),
            pl.BlockSpec((1, NT, IDX_DIM, TK), lambda b, q: (b, 0, 0, 0)),
            pl.BlockSpec((1, NT, C_KV_RANK, TK), lambda b, q: (b, 0, 0, 0)),
            pl.BlockSpec((1, NT, TK, C_KV_RANK), lambda b, q: (b, 0, 0, 0)),
            full((H, C_Q_RANK, C_HEAD_DIM)),
            full((H, C_HEAD_DIM, C_KV_RANK)),
            full((H, C_KV_RANK, LANES)),
        ],
        out_specs=pl.BlockSpec((1, QB, H * C_HEAD_DIM), lambda b, q: (b, q, 0)),
        scratch_shapes=[
            pltpu.VMEM((NT, QB, TK), jnp.int32),
            pltpu.VMEM((IDX_HEADS, QB, LANES), jnp.float32),
            pltpu.VMEM((H * QB, C_KV_RANK), bf),
            pltpu.VMEM((QB, TK), jnp.float32),
            pltpu.VMEM((H * QB, LANES), jnp.float32),
            pltpu.VMEM((H * QB, LANES), jnp.float32),
            pltpu.VMEM((H * QB, C_KV_RANK), jnp.float32),
            pltpu.VMEM((QB, LANES), jnp.int32),
            pltpu.VMEM((QB, LANES), jnp.int32),
        ],
    )
    return pl.pallas_call(
        kern,
        grid_spec=grid_spec,
        out_shape=jax.ShapeDtypeStruct((B, T, H * C_HEAD_DIM), jnp.float32),
        compiler_params=pltpu.CompilerParams(dimension_semantics=("arbitrary", "arbitrary"),
                                             vmem_limit_bytes=56 * 1024 * 1024),
        name="dsa_attention",
    )(slopes2, cq, qidx, w_idx, kidxT, ckvT, ckv, wuq, wukT, wuvp)


def _mixer_c(x, w_in, q_norm, kv_norm, w_uq, w_uk, w_uv):
    return _dsa_attention(*_dsa_proj(x, w_in, q_norm, kv_norm), w_uq, w_uk, w_uv)


def _moe_layer(x1, route, cnt, routeT, p, w_gu, w_down, ln_g, ln_b, w_gate, w_proj):
    N = x1.shape[0]
    M = N * TOPK_IN_GROUP
    flat_e = routeT[0:TOPK_IN_GROUP].astype(jnp.int32).T.reshape(M)
    rank = routeT[4:4 + TOPK_IN_GROUP].astype(jnp.int32).T.reshape(M)
    counts = cnt[0, N_GROUPS:N_GROUPS + N_EXPERTS].astype(jnp.int32)
    return _moe_experts(x1, flat_e, rank, counts, route, p, w_gu, w_down, ln_g, ln_b, w_gate, w_proj)


MOE_BLK = 256
MOE_TM = 256
MOE_DMA_UNROLL = 8


def _dispatch_kernel(dest_ref, x_ref, xs_init_ref, xs_ref, sem):
    del xs_init_ref
    base = pl.program_id(0) * (TOPK_IN_GROUP * MOE_TM)

    def row_copy(t, k):
        d = dest_ref[base + TOPK_IN_GROUP * t + k]
        return pltpu.make_async_copy(x_ref.at[pl.ds(t, 1), :], xs_ref.at[pl.ds(d, 1), :], sem)

    def issue(t, c):
        for k in range(TOPK_IN_GROUP):
            row_copy(t, k).start(priority=k)
        return c

    def drain(t, c):
        for k in range(TOPK_IN_GROUP):
            row_copy(t, k).wait()
        return c

    lax.fori_loop(0, MOE_TM, issue, 0, unroll=MOE_DMA_UNROLL)
    lax.fori_loop(0, MOE_TM, drain, 0, unroll=MOE_DMA_UNROLL)


def _expert_kernel(be_ref, nu_ref, xs_ref, wgu_ref, wd_ref, ys_ref, wgu_bf, wd_bf):
    b = pl.program_id(0)
    F = wd_bf.shape[0]

    @pl.when(b < nu_ref[0])
    def _():
        e = be_ref[b]
        prev = be_ref[jnp.maximum(b - 1, 0)]

        @pl.when((b == 0) | (e != prev))
        def _():
            wgu_bf[...] = wgu_ref[0].astype(jnp.bfloat16)
            wd_bf[...] = wd_ref[0].astype(jnp.bfloat16)

        h = jnp.dot(xs_ref[...].astype(jnp.bfloat16), wgu_bf[...], preferred_element_type=jnp.float32)
        a = h[:, :F]
        act = a * jax.nn.sigmoid(a) * h[:, F:]
        ys_ref[...] = jnp.dot(act.astype(jnp.bfloat16), wd_bf[...], preferred_element_type=jnp.float32)

    @pl.when(b >= nu_ref[0])
    def _():
        ys_ref[...] = jnp.zeros(ys_ref.shape, ys_ref.dtype)


def _combine_kernel(dest_ref, ys_ref, route_ref, x1_ref, p_ref, lng_ref, lnb_ref, wg_ref, wp_ref,
                    o_ref, buf, sem):
    i = pl.program_id(0)
    n = pl.num_programs(0)

    def row_copy(step, slot, t, k):
        d = dest_ref[step * (TOPK_IN_GROUP * MOE_TM) + TOPK_IN_GROUP * t + k]
        return pltpu.make_async_copy(ys_ref.at[pl.ds(d, 1), :], buf.at[slot, k, pl.ds(t, 1), :], sem.at[slot])

    def issue(step, slot):
        def body(t, c):
            for k in range(TOPK_IN_GROUP):
                row_copy(step, slot, t, k).start(priority=k)
            return c
        lax.fori_loop(0, MOE_TM, body, 0, unroll=MOE_DMA_UNROLL)

    def drain(step, slot):
        def body(t, c):
            for k in range(TOPK_IN_GROUP):
                row_copy(step, slot, t, k).wait()
            return c
        lax.fori_loop(0, MOE_TM, body, 0, unroll=MOE_DMA_UNROLL)

    @pl.when(i == 0)
    def _():
        issue(0, 0)

    @pl.when(i + 1 < n)
    def _():
        issue(i + 1, (i + 1) % 2)

    slot = i % 2
    drain(i, slot)
    g = route_ref[...]
    m = g[:, 2:3] * buf[slot, 0] + g[:, 3:4] * buf[slot, 1]
    x2 = _ln_rows(ALPHA * x1_ref[...] + m, lng_ref[...], lnb_ref[...])
    o_ref[...] = x2 + jax.nn.sigmoid(_bdot(x2, wg_ref[...])) * _bdot(p_ref[...], wp_ref[...])


def _moe_experts(xt, flat_e, rank, counts, route, p, w_gu, w_down, ln_g, ln_b, w_gate, w_proj):
    N, D = xt.shape
    M = flat_e.shape[0]
    F = w_down.shape[1]
    BLK, TM = MOE_BLK, MOE_TM
    nb = -(-(M + N_EXPERTS * (BLK - 1)) // BLK)
    R = nb * BLK
    padded = ((counts + BLK - 1) // BLK) * BLK
    pend = jnp.cumsum(padded)
    pstart = pend - padded
    dest = (pstart[flat_e] + rank).astype(jnp.int32)
    nused = (pend[-1:] // BLK).astype(jnp.int32)
    blk_e = jnp.minimum(jnp.sum(pend[None, :] <= (jnp.arange(nb) * BLK)[:, None], axis=1),
                        N_EXPERTS - 1).astype(jnp.int32)
    vmem = 56 * 1024 * 1024

    xs = pl.pallas_call(
        _dispatch_kernel,
        grid_spec=pltpu.PrefetchScalarGridSpec(
            num_scalar_prefetch=1,
            grid=(N // TM,),
            in_specs=[pl.BlockSpec((TM, D), lambda i, d: (i, 0)),
                      pl.BlockSpec(memory_space=pl.ANY)],
            out_specs=pl.BlockSpec(memory_space=pl.ANY),
            scratch_shapes=[pltpu.SemaphoreType.DMA(())],
        ),
        out_shape=jax.ShapeDtypeStruct((R, D), jnp.float32),
        input_output_aliases={2: 0},
        compiler_params=pltpu.CompilerParams(dimension_semantics=("arbitrary",), vmem_limit_bytes=vmem),
        name="moe_dispatch",
    )(dest, xt, jnp.zeros((R, D), jnp.float32))

    def blk(b, be, nu):
        return jnp.minimum(b, nu[0] - 1)

    ys = pl.pallas_call(
        _expert_kernel,
        grid_spec=pltpu.PrefetchScalarGridSpec(
            num_scalar_prefetch=2,
            grid=(nb,),
            in_specs=[pl.BlockSpec((BLK, D), lambda b, be, nu: (blk(b, be, nu), 0)),
                      pl.BlockSpec((1, D, 2 * F), lambda b, be, nu: (be[blk(b, be, nu)], 0, 0)),
                      pl.BlockSpec((1, F, D), lambda b, be, nu: (be[blk(b, be, nu)], 0, 0))],
            out_specs=pl.BlockSpec((BLK, D), lambda b, be, nu: (b, 0)),
            scratch_shapes=[pltpu.VMEM((D, 2 * F), jnp.bfloat16), pltpu.VMEM((F, D), jnp.bfloat16)],
        ),
        out_shape=jax.ShapeDtypeStruct((R, D), jnp.float32),
        compiler_params=pltpu.CompilerParams(dimension_semantics=("arbitrary",), vmem_limit_bytes=vmem),
        name="moe_experts",
    )(blk_e, nused, xs, w_gu, w_down)

    return pl.pallas_call(
        _combine_kernel,
        grid_spec=pltpu.PrefetchScalarGridSpec(
            num_scalar_prefetch=1,
            grid=(N // TM,),
            in_specs=[pl.BlockSpec(memory_space=pl.ANY),
                      pl.BlockSpec((TM, ROUTE_W), lambda i, d: (i, 0)),
                      pl.BlockSpec((TM, D), lambda i, d: (i, 0)),
                      pl.BlockSpec((TM, p.shape[1]), lambda i, d: (i, 0)),
                      pl.BlockSpec((1, D), lambda i, d: (0, 0)),
                      pl.BlockSpec((1, D), lambda i, d: (0, 0)),
                      pl.BlockSpec((D, D), lambda i, d: (0, 0)),
                      pl.BlockSpec((p.shape[1], D), lambda i, d: (0, 0))],
            out_specs=pl.BlockSpec((TM, D), lambda i, d: (i, 0)),
            scratch_shapes=[pltpu.VMEM((2, TOPK_IN_GROUP, TM, D), jnp.float32),
                            pltpu.SemaphoreType.DMA((2,))],
        ),
        out_shape=jax.ShapeDtypeStruct((N, D), jnp.float32),
        compiler_params=pltpu.CompilerParams(dimension_semantics=("arbitrary",), vmem_limit_bytes=vmem),
        name="moe_combine",
    )(dest, ys, route, xt, p, ln_g[None, :], ln_b[None, :],
      w_gate.astype(jnp.bfloat16), w_proj.astype(jnp.bfloat16))


def kernel(x, p, ab_w_in, ret_gn_g, pool_w, pool_scale, ab_w_out, c_w_in, c_q_norm, c_kv_norm, c_w_uq, c_w_uk, c_w_uv, c_w_out, ln1_g, ln1_b, ln2_g, ln2_b, moe_w_group, moe_b_group, moe_w_expert, moe_b_expert, moe_w_gu, moe_w_down, ple_w_gate, ple_w_proj):
    B, T, D = x.shape
    N = B * T
    for i in range(DEPTH):
        j = i // 2
        wr, br = _router_weights(moe_w_group[i], moe_b_group[i], moe_w_expert[i], moe_b_expert[i])
        if i % 2 == 0:
            x1, route, cnt, routeT = _mixer_ab_layer(x, ab_w_in[j], ret_gn_g[j], pool_w[j], pool_scale[j],
                                                     ab_w_out[j], ln1_g[i], ln1_b[i], wr, br)
            x1, route = x1.reshape(N, D), route.reshape(N, ROUTE_W)
        else:
            o = _mixer_c(x, c_w_in[j], c_q_norm[j], c_kv_norm[j], c_w_uq[j], c_w_uk[j], c_w_uv[j])
            x1, route, cnt, routeT = _outproj_layer(o.reshape(N, -1), x.reshape(N, D), c_w_out[j],
                                                    ln1_g[i], ln1_b[i], wr, br)
        x = _moe_layer(x1, route, cnt, routeT, p[i].reshape(N, -1), moe_w_gu[i], moe_w_down[i],
                       ln2_g[i], ln2_b[i], ple_w_gate[i], ple_w_proj[i]).reshape(B, T, D)
    return x
```

```python
import functools
import math
import jax
import jax.numpy as jnp
from jax import lax
from jax.experimental import pallas as pl
from jax.experimental.pallas import tpu as pltpu

D_MODEL = 1024
DEPTH = 2
RET_HEADS = 4
RET_DK = 128
RET_DV = 128
RET_CHUNK = 128
POOL_WINDOWS = (2, 4, 8, 16)
POOL_GROUP = 128
C_HEADS = 16
C_HEAD_DIM = 64
C_Q_RANK = 256
C_KV_RANK = 256
IDX_HEADS = 8
IDX_DIM = 64
IDX_TOPK_MAX = 256
Q_BLOCK = 128
N_GROUPS = 4
EXPERTS_PER_GROUP = 8
N_EXPERTS = N_GROUPS * EXPERTS_PER_GROUP
TOPK_IN_GROUP = 2
MOE_BLOCK = 128
ALPHA = (2.0 * DEPTH) ** 0.25
LN_EPS = 1e-5


def _mm_kernel(x_ref, w_ref, o_ref):
    o_ref[...] = jnp.dot(x_ref[...].astype(jnp.bfloat16), w_ref[...].astype(jnp.bfloat16),
                         preferred_element_type=jnp.float32)


def _matmul(x, w, tm=512):
    M, K = x.shape
    N = w.shape[1]
    npad = (-N) % 128
    if npad:
        w = jnp.pad(w, ((0, 0), (0, npad)))
    Np = N + npad
    out = pl.pallas_call(
        _mm_kernel,
        grid=(M // tm,),
        in_specs=[pl.BlockSpec((tm, K), lambda i: (i, 0)),
                  pl.BlockSpec((K, Np), lambda i: (0, 0))],
        out_specs=pl.BlockSpec((tm, Np), lambda i: (i, 0)),
        out_shape=jax.ShapeDtypeStruct((M, Np), jnp.float32),
        compiler_params=pltpu.CompilerParams(dimension_semantics=("arbitrary",),
                                             vmem_limit_bytes=56 * 1024 * 1024),
    )(x, w)
    return out[:, :N] if npad else out


def _mm3(x, w):
    B, T, K = x.shape
    return _matmul(x.reshape(B * T, K), w).reshape(B, T, w.shape[1])


def _layer_norm(x, g, b):
    mu = x.mean(-1, keepdims=True)
    var = jnp.square(x - mu).mean(-1, keepdims=True)
    return (x - mu) * lax.rsqrt(var + LN_EPS) * g + b


def _rms_norm(x, g):
    return x * lax.rsqrt(jnp.mean(x * x, -1, keepdims=True) + 1e-6) * g


TOK_TM = 512
POOL_TAIL = 16
VMEM_LIMIT = 56 * 1024 * 1024
ROUTE_W = 128
ROUTE_ROWS = 8


def _bdot(a, b):
    return jnp.dot(a.astype(jnp.bfloat16), b.astype(jnp.bfloat16), preferred_element_type=jnp.float32)


def _ln_rows(v, g, b):
    mu = jnp.mean(v, axis=1, keepdims=True)
    d = v - mu
    var = jnp.mean(d * d, axis=1, keepdims=True)
    return d * lax.rsqrt(var + LN_EPS) * g + b


def _route_rows(x1, wr_ref, br_ref, tri_ref, run_scr, cnt_ref):
    f32 = jnp.float32
    logits = _bdot(x1, wr_ref[...]) + br_ref[...]
    lane = lax.broadcasted_iota(jnp.int32, logits.shape, 1)
    lane_f = lane.astype(f32)
    ninf = -jnp.inf
    gl = jnp.where(lane < N_GROUPS, logits, ninf)
    gmax = jnp.max(gl, axis=1, keepdims=True)
    grp = jnp.min(jnp.where(gl == gmax, lane_f, float(ROUTE_W)), axis=1, keepdims=True)
    g_w = 1.0 / jnp.sum(jnp.exp(gl - gmax), axis=1, keepdims=True)
    egrp = ((lane - N_GROUPS) >> 3).astype(f32)
    el = jnp.where(lane >= N_GROUPS, jnp.where(egrp == grp, logits, ninf), ninf)
    v1 = jnp.max(el, axis=1, keepdims=True)
    i1 = jnp.min(jnp.where(el == v1, lane_f, float(ROUTE_W)), axis=1, keepdims=True)
    el2 = jnp.where(lane_f == i1, ninf, el)
    v2 = jnp.max(el2, axis=1, keepdims=True)
    i2 = jnp.min(jnp.where(el2 == v2, lane_f, float(ROUTE_W)), axis=1, keepdims=True)
    e = jnp.exp(v2 - v1)
    den = 1.0 / (1.0 + e)
    oh1 = jnp.where(lane_f == i1, 1.0, 0.0)
    oh2 = jnp.where(lane_f == i2, 1.0, 0.0)
    oh = oh1 + oh2
    base = run_scr[0:1, :] + jnp.dot(tri_ref[...], oh.astype(jnp.bfloat16), preferred_element_type=f32)
    r1 = jnp.sum(oh1 * base, axis=1, keepdims=True)
    r2 = jnp.sum(oh2 * base, axis=1, keepdims=True)
    run_new = run_scr[0:1, :] + jnp.sum(oh, axis=0, keepdims=True)
    run_scr[0:1, :] = run_new
    cnt_ref[...] = jnp.broadcast_to(run_new, cnt_ref.shape)
    sel = lambda l, v, rest: jnp.where(lane == l, v, rest)
    return sel(0, i1 - N_GROUPS, sel(1, i2 - N_GROUPS, sel(2, g_w * den, sel(3, g_w * e * den,
                                                                             sel(4, r1, sel(5, r2, 0.0))))))


def _mixer_ab_kernel(cdec_ref, x_ref, win_ref, idec_ref, qdec_ref, kdec_ref, gng_ref, bandc_ref, bandp_ref,
                     pw_ref, psc_ref, wout_ref, lng_ref, lnb_ref, wr_ref, br_ref, tri_ref,
                     x1_ref, route_ref, cnt_ref, routeT_ref, r_scr, tail_scr, mixed_scr, run_scr):
    f32 = jnp.float32
    bf = jnp.bfloat16
    C = RET_CHUNK
    TM = x_ref.shape[1]
    nq = RET_HEADS * RET_DK
    nv = RET_HEADS * RET_DV
    ti = pl.program_id(1)

    @pl.when(ti == 0)
    def _():
        r_scr[...] = jnp.zeros(r_scr.shape, f32)
        tail_scr[...] = jnp.zeros(tail_scr.shape, f32)

    x = x_ref[0]
    xb = x.astype(bf)
    zq = jnp.dot(xb, win_ref[:, 0:nq], preferred_element_type=f32)
    zk = jnp.dot(xb, win_ref[:, nq:2 * nq], preferred_element_type=f32) * (RET_DK ** -0.5)
    zv = jnp.dot(xb, win_ref[:, 2 * nq:2 * nq + nv], preferred_element_type=f32)
    zg = jnp.dot(xb, win_ref[:, 2 * nq + nv:2 * nq + 2 * nv], preferred_element_type=f32)
    zu = jnp.dot(xb, win_ref[:, 2 * nq + 2 * nv:], preferred_element_type=f32)

    for h in range(RET_HEADS):
        hs = slice(h * RET_DK, (h + 1) * RET_DK)
        R = r_scr[h]
        for c in range(TM // C):
            cs = slice(c * C, (c + 1) * C)
            qc, kc, vc = zq[cs, hs], zk[cs, hs], zv[cs, hs]
            s = lax.dot_general(qc.astype(bf), kc.astype(bf), (((1,), (1,)), ((), ())),
                                preferred_element_type=f32) * idec_ref[h]
            o = _bdot(s, vc) + _bdot(qc * qdec_ref[h], R)
            R = R * cdec_ref[h] + lax.dot_general((kc * kdec_ref[h]).astype(bf), vc.astype(bf),
                                                  (((0,), (0,)), ((), ())), preferred_element_type=f32)
            mu = jnp.mean(o, axis=1, keepdims=True)
            d = o - mu
            var = jnp.mean(d * d, axis=1, keepdims=True)
            gate = zg[cs, hs]
            ret = d * lax.rsqrt(var + LN_EPS) * gng_ref[:, hs] * (gate * jax.nn.sigmoid(gate))
            mixed_scr[cs, hs] = ret.astype(bf)
        r_scr[h] = R

    t_glob = ti * TM + lax.broadcasted_iota(jnp.int32, (TM, POOL_GROUP), 0)
    tail = tail_scr[...]
    for gi, w in enumerate(POOL_WINDOWS):
        gs = slice(gi * POOL_GROUP, (gi + 1) * POOL_GROUP)
        u = zu[:, gs]
        u_hi = u.astype(bf)
        u_lo = (u - u_hi.astype(f32)).astype(bf)
        tl = tail[:, gs]
        t_hi = tl.astype(bf)
        t_lo = (tl - t_hi.astype(f32)).astype(bf)
        wsum = (jnp.dot(bandc_ref[gi], u_hi, preferred_element_type=f32)
                + jnp.dot(bandc_ref[gi], u_lo, preferred_element_type=f32)
                + jnp.dot(bandp_ref[gi], t_hi, preferred_element_type=f32)
                + jnp.dot(bandp_ref[gi], t_lo, preferred_element_type=f32))
        cnt = jnp.minimum(t_glob + 1, w).astype(f32)
        dpool = wsum / cnt - u
        y = _bdot(dpool, pw_ref[gi]) * psc_ref[:, gs]
        mixed_scr[:, nv + gi * POOL_GROUP:nv + (gi + 1) * POOL_GROUP] = y.astype(bf)
    tail_scr[...] = zu[TM - POOL_TAIL:, :]

    hmix = jnp.dot(mixed_scr[...], wout_ref[...], preferred_element_type=f32)
    x1 = _ln_rows(ALPHA * x + hmix, lng_ref[...], lnb_ref[...])
    x1_ref[0] = x1
    @pl.when((pl.program_id(0) == 0) & (ti == 0))
    def _():
        run_scr[...] = jnp.zeros(run_scr.shape, f32)

    route = _route_rows(x1, wr_ref, br_ref, tri_ref, run_scr, cnt_ref)
    route_ref[0] = route
    routeT_ref[...] = route.T[0:ROUTE_ROWS, :]


def _const_spec(shape):
    return pl.BlockSpec(shape, lambda b, t: (0,) * len(shape))


def _router_weights(w_group, b_group, w_expert, b_expert):
    D = w_group.shape[0]
    pad = ROUTE_W - N_GROUPS - N_EXPERTS
    wr = jnp.concatenate([w_group, w_expert, jnp.zeros((D, pad), jnp.float32)], axis=1).astype(jnp.bfloat16)
    br = jnp.concatenate([b_group, b_expert, jnp.zeros((pad,), jnp.float32)])[None, :]
    return wr, br


def _mixer_ab_layer(x, w_in, gn_g, pool_w, pool_scale, w_out, ln_g, ln_b, wr, br):
    B, T, D = x.shape
    TM = TOK_TM
    C = RET_CHUNK
    H = RET_HEADS
    bf = jnp.bfloat16
    log_g = jnp.log1p(-jnp.exp2(-5.0 - jnp.arange(H, dtype=jnp.float32)))
    i = jnp.arange(C, dtype=jnp.float32)
    diff = i[:, None] - i[None, :]
    idec = jnp.where(diff >= 0, jnp.exp(log_g[:, None, None] * jnp.maximum(diff, 0.0)), 0.0)
    qdec = jnp.broadcast_to(jnp.exp(log_g[:, None] * (i + 1.0))[:, :, None], (H, C, RET_DK))
    kdec = jnp.broadcast_to(jnp.exp(log_g[:, None] * (C - 1.0 - i))[:, :, None], (H, C, RET_DK))
    cdec = jnp.exp(log_g * C)
    r = jnp.arange(TM)[:, None]
    bandc = jnp.stack([((r - jnp.arange(TM)[None, :] >= 0) & (r - jnp.arange(TM)[None, :] < w))
                       for w in POOL_WINDOWS]).astype(bf)
    jp = jnp.arange(POOL_TAIL)[None, :] - POOL_TAIL
    bandp = jnp.stack([((r - jp) < w) for w in POOL_WINDOWS]).astype(bf)
    nG = len(POOL_WINDOWS)
    return pl.pallas_call(
        _mixer_ab_kernel,
        grid=(B, T // TM),
        in_specs=[pl.BlockSpec(memory_space=pltpu.SMEM),
                  pl.BlockSpec((1, TM, D), lambda b, t: (b, t, 0)),
                  _const_spec(w_in.shape), _const_spec((H, C, C)), _const_spec((H, C, RET_DK)),
                  _const_spec((H, C, RET_DK)), _const_spec((1, H * RET_DV)),
                  _const_spec((nG, TM, TM)), _const_spec((nG, TM, POOL_TAIL)),
                  _const_spec((nG, POOL_GROUP, POOL_GROUP)), _const_spec((1, nG * POOL_GROUP)),
                  _const_spec(w_out.shape), _const_spec((1, D)), _const_spec((1, D)),
                  _const_spec((D, ROUTE_W)), _const_spec((1, ROUTE_W)), _const_spec((TM, TM))],
        out_specs=[pl.BlockSpec((1, TM, D), lambda b, t: (b, t, 0)),
                   pl.BlockSpec((1, TM, ROUTE_W), lambda b, t: (b, t, 0)),
                   _const_spec((8, ROUTE_W)),
                   pl.BlockSpec((ROUTE_ROWS, TM), lambda b, t: (0, b * (T // TM) + t))],
        out_shape=[jax.ShapeDtypeStruct((B, T, D), jnp.float32),
                   jax.ShapeDtypeStruct((B, T, ROUTE_W), jnp.float32),
                   jax.ShapeDtypeStruct((8, ROUTE_W), jnp.float32),
                   jax.ShapeDtypeStruct((ROUTE_ROWS, B * T), jnp.float32)],
        scratch_shapes=[pltpu.VMEM((H, RET_DK, RET_DV), jnp.float32),
                        pltpu.VMEM((POOL_TAIL, nG * POOL_GROUP), jnp.float32),
                        pltpu.VMEM((TM, w_out.shape[0]), bf),
                        pltpu.VMEM((8, ROUTE_W), jnp.float32)],
        compiler_params=pltpu.CompilerParams(dimension_semantics=("arbitrary", "arbitrary"),
                                             vmem_limit_bytes=VMEM_LIMIT),
        name="mixer_ab_layer",
    )(cdec, x, w_in.astype(bf), idec, qdec, kdec, gn_g[None, :], bandc, bandp,
      pool_w.astype(bf), pool_scale[None, :], w_out.astype(bf), ln_g[None, :], ln_b[None, :], wr, br,
      _strict_lower_tri(TM))


def _outproj_kernel(o_ref, x_ref, wout_ref, lng_ref, lnb_ref, wr_ref, br_ref, tri_ref,
                    x1_ref, route_ref, cnt_ref, routeT_ref, run_scr):
    @pl.when(pl.program_id(0) == 0)
    def _():
        run_scr[...] = jnp.zeros(run_scr.shape, jnp.float32)

    h = _bdot(o_ref[...], wout_ref[...])
    x1 = _ln_rows(ALPHA * x_ref[...] + h, lng_ref[...], lnb_ref[...])
    x1_ref[...] = x1
    route = _route_rows(x1, wr_ref, br_ref, tri_ref, run_scr, cnt_ref)
    route_ref[...] = route
    routeT_ref[...] = route.T[0:ROUTE_ROWS, :]


def _strict_lower_tri(n):
    r = jnp.arange(n)
    return (r[:, None] > r[None, :]).astype(jnp.bfloat16)


def _outproj_layer(o, x, w_out, ln_g, ln_b, wr, br):
    N, D = x.shape
    TM = TOK_TM
    cs = lambda shape: pl.BlockSpec(shape, lambda i: (0,) * len(shape))
    return pl.pallas_call(
        _outproj_kernel,
        grid=(N // TM,),
        in_specs=[pl.BlockSpec((TM, o.shape[1]), lambda i: (i, 0)),
                  pl.BlockSpec((TM, D), lambda i: (i, 0)),
                  cs(w_out.shape), cs((1, D)), cs((1, D)), cs((D, ROUTE_W)), cs((1, ROUTE_W)),
                  cs((TM, TM))],
        out_specs=[pl.BlockSpec((TM, D), lambda i: (i, 0)),
                   pl.BlockSpec((TM, ROUTE_W), lambda i: (i, 0)),
                   cs((8, ROUTE_W)),
                   pl.BlockSpec((ROUTE_ROWS, TM), lambda i: (0, i))],
        out_shape=[jax.ShapeDtypeStruct((N, D), jnp.float32),
                   jax.ShapeDtypeStruct((N, ROUTE_W), jnp.float32),
                   jax.ShapeDtypeStruct((8, ROUTE_W), jnp.float32),
                   jax.ShapeDtypeStruct((ROUTE_ROWS, N), jnp.float32)],
        scratch_shapes=[pltpu.VMEM((8, ROUTE_W), jnp.float32)],
        compiler_params=pltpu.CompilerParams(dimension_semantics=("arbitrary",),
                                             vmem_limit_bytes=VMEM_LIMIT),
        name="outproj_layer",
    )(o, x, w_out.astype(jnp.bfloat16), ln_g[None, :], ln_b[None, :], wr, br, _strict_lower_tri(TM))


LANES = 128
QB = Q_BLOCK
DSA_TK = 512
DSA_HG = 4
DSA_VALUE_STEPS = 24
DSA_ROUNDS_PER_TEST = 4
NEG_BIG = -1e30
KEY_LO0 = -2139095040
KEY_HI0 = 2139095040
LOG2E = 1.4426950408889634
BIG_IDX = 1 << 30


def _rep(x, n, axis):
    return jnp.concatenate([x] * n, axis=axis) if n > 1 else x


def _sum_lane_chunks(v, width):
    out = v[:, 0:LANES]
    for c in range(1, width // LANES):
        out = out + v[:, c * LANES:(c + 1) * LANES]
    return out


def _dsa_kernel(slopes_ref, cq_ref, qidx_ref, widx_ref, kidxT_ref, ckvT_ref, ckv_ref,
                wuq_ref, wukT_ref, wuvp_ref, o_ref,
                keys_scr, wb_scr, q_scr, a_scr, m_scr, l_scr, acc_scr,
                th_scr, c_scr, *, topk, seq_len):
    f32 = jnp.float32
    TK = DSA_TK
    NCH = TK // LANES
    H = C_HEADS
    qb = pl.program_id(1)
    t0 = qb * QB
    n_kt = (t0 + QB - 1) // TK + 1
    row = lax.broadcasted_iota(jnp.int32, (QB, TK), 0)
    col = lax.broadcasted_iota(jnp.int32, (QB, TK), 1)
    kf = float(topk)

    for h in range(IDX_HEADS):
        wb_scr[h] = jnp.broadcast_to(widx_ref[0, :, h:h + 1], (QB, LANES))

    def idx_body(j, carry):
        kT = kidxT_ref[0, j]
        s_all = jnp.dot(qidx_ref[0].reshape(IDX_HEADS * QB, IDX_DIM), kT, preferred_element_type=f32)
        acc = jnp.zeros((QB, TK), f32)
        for h in range(IDX_HEADS):
            acc = acc + _rep(wb_scr[h], NCH, axis=1) * jnp.maximum(s_all[h * QB:(h + 1) * QB, :], 0.0)
        causal = (j * TK + col) <= (t0 + row)
        sc = jnp.where(causal, acc, -jnp.inf)
        b = pltpu.bitcast(sc, jnp.int32)
        keys_scr[j] = b ^ ((b >> 31) & 0x7FFFFFFF)
        vmax, vmin = carry
        sc_hi = jnp.where(causal, acc, jnp.inf)
        for c in range(NCH):
            vmax = jnp.maximum(vmax, sc[:, c * LANES:(c + 1) * LANES])
            vmin = jnp.minimum(vmin, sc_hi[:, c * LANES:(c + 1) * LANES])
        return vmax, vmin

    vmax, vmin = lax.fori_loop(0, n_kt, idx_body, (jnp.full((QB, LANES), -jnp.inf, f32),
                                                   jnp.full((QB, LANES), jnp.inf, f32)))

    def to_key(v):
        b = pltpu.bitcast(v, jnp.int32)
        return b ^ ((b >> 31) & 0x7FFFFFFF)

    def from_key(k):
        return pltpu.bitcast(k ^ ((k >> 31) & 0x7FFFFFFF), f32)

    @pl.when(n_kt < keys_scr.shape[0])
    def _():
        keys_scr[n_kt] = jnp.full((QB, TK), KEY_LO0 - 1, jnp.int32)

    def count_where(ones_fn):
        def body(j2, c):
            j = 2 * j2
            return (c + _sum_lane_chunks(ones_fn(j, keys_scr[j]), TK)
                    + _sum_lane_chunks(ones_fn(j + 1, keys_scr[j + 1]), TK))
        c = lax.fori_loop(0, (n_kt + 1) // 2, body, jnp.zeros((QB, LANES), f32))
        return jnp.broadcast_to(jnp.sum(c, axis=1, keepdims=True), (QB, LANES))

    n_valid = (t0 + 1 + lax.broadcasted_iota(jnp.int32, (QB, LANES), 0)).astype(f32)
    small = n_valid <= kf
    kmax = to_key(jnp.broadcast_to(jnp.max(vmax, axis=1, keepdims=True), (QB, LANES)))
    kmin = to_key(jnp.broadcast_to(jnp.min(vmin, axis=1, keepdims=True), (QB, LANES)))
    lo0 = jnp.where(small, KEY_LO0, kmin)
    hi0 = jnp.where(small, KEY_LO0 + 1, kmax + 1)

    def row_done(lo, hi, clo):
        return jnp.where(clo == kf, 1.0, jnp.where(hi == lo + 1, 1.0, 0.0))

    def search_round(mid_fn, lo, hi, clo, chi):
        mid = jnp.minimum(jnp.maximum(mid_fn(lo, hi), lo + 1), hi - 1)
        midb = _rep(mid, NCH, axis=1)
        cnt = count_where(lambda j, k: jnp.where(k >= midb, 1.0, 0.0))
        ge = jnp.where(small, kf, cnt) >= kf
        return jnp.where(ge, mid, lo), jnp.where(ge, hi, mid), jnp.where(ge, cnt, clo), jnp.where(ge, chi, cnt)

    def search(mid_fn, max_rounds, st):
        def cond(s):
            return (s[1] > 0) & (s[0] < max_rounds)

        def body(s):
            it, _, lo, hi, clo, chi = s
            for _ in range(DSA_ROUNDS_PER_TEST):
                lo, hi, clo, chi = search_round(mid_fn, lo, hi, clo, chi)
            pending = (jnp.min(row_done(lo, hi, clo)) < 0.5).astype(jnp.int32)
            return it + DSA_ROUNDS_PER_TEST, pending, lo, hi, clo, chi

        return lax.while_loop(cond, body, st)

    pending0 = (jnp.min(row_done(lo0, hi0, n_valid)) < 0.5).astype(jnp.int32)
    st = (jnp.int32(0), pending0, lo0, hi0, n_valid, jnp.zeros((QB, LANES), f32))
    st = search(lambda lo, hi: to_key(0.5 * from_key(lo) + 0.5 * from_key(hi - 1)), DSA_VALUE_STEPS, st)
    st = search(lambda lo, hi: (lo & hi) + ((lo ^ hi) >> 1), 32 + DSA_ROUNDS_PER_TEST, (jnp.int32(0),) + st[1:])
    _, _, theta, _, n_ge, n_gt = st
    th_scr[...] = theta
    thb = _rep(theta, NCH, axis=1)

    need = kf - n_gt
    excess = n_ge > kf
    c_scr[...] = jnp.full((QB, LANES), BIG_IDX, jnp.int32)

    @pl.when(jnp.max(jnp.where(excess, 1.0, 0.0)) > 0.0)
    def _():
        def tie_body(i, carry):
            lo, hi = carry
            mid = (lo + hi) >> 1
            midb = _rep(mid, NCH, axis=1)
            cnt = count_where(lambda j, k: jnp.where(
                k == thb, jnp.where((j * TK + col) <= midb, 1.0, 0.0), 0.0))
            ok = cnt >= need
            return jnp.where(ok, lo, mid + 1), jnp.where(ok, mid, hi)
        nbits = max(1, (seq_len - 1).bit_length())
        lo, _ = lax.fori_loop(0, nbits, tie_body,
                              (jnp.zeros((QB, LANES), jnp.int32),
                               jnp.full((QB, LANES), seq_len - 1, jnp.int32)))
        c_scr[...] = jnp.where(excess, lo, BIG_IDX)

    cq = cq_ref[0]
    for h in range(H):
        qh = jnp.dot(cq, wuq_ref[h], preferred_element_type=f32).astype(jnp.bfloat16)
        qa = jnp.dot(qh, wukT_ref[h], preferred_element_type=f32) * (C_HEAD_DIM ** -0.5 * LOG2E)
        q_scr[h * QB:(h + 1) * QB, :] = qa.astype(jnp.bfloat16)

    m_scr[...] = jnp.full(m_scr.shape, NEG_BIG, f32)
    l_scr[...] = jnp.zeros(l_scr.shape, f32)
    acc_scr[...] = jnp.zeros(acc_scr.shape, f32)

    def att_body(j, carry):
        kT = ckvT_ref[0, j]
        kv = ckv_ref[0, j]
        k = keys_scr[j]
        thb_ = _rep(th_scr[...], NCH, axis=1)
        cb_ = _rep(c_scr[...], NCH, axis=1)
        spos = j * TK + col
        keep_tie = jnp.where(spos <= cb_, 0.0, NEG_BIG)
        a_scr[...] = jnp.where(k > thb_, 0.0, jnp.where(k == thb_, keep_tie, NEG_BIG))
        srel8 = (j * TK - t0 + lax.broadcasted_iota(jnp.int32, (8, TK), 1)).astype(f32)

        for g in range(H // DSA_HG):
            g0 = g * DSA_HG * QB
            lg = jnp.dot(q_scr[g0:g0 + DSA_HG * QB, :], kT, preferred_element_type=f32)
            ps, alphas = [], []
            for u in range(DSA_HG):
                h = g * DSA_HG + u
                r = slice(h * QB, (h + 1) * QB)
                bh = _rep(slopes_ref[h] * srel8, QB // 8, axis=0)
                l = lg[u * QB:(u + 1) * QB, :] + a_scr[...] + bh
                m_old = m_scr[r, :]
                m_new = jnp.maximum(m_old, jnp.max(l, axis=1, keepdims=True))
                p = jnp.exp2(l - _rep(m_new, NCH, axis=1))
                alpha = jnp.exp2(m_old - m_new)
                rs = jnp.sum(_sum_lane_chunks(p, TK), axis=1, keepdims=True)
                l_scr[r, :] = alpha * l_scr[r, :] + rs
                m_scr[r, :] = m_new
                ps.append(p.astype(jnp.bfloat16))
                alphas.append(_rep(alpha, C_KV_RANK // LANES, axis=1))
            pv = jnp.dot(jnp.concatenate(ps, axis=0), kv, preferred_element_type=f32)
            acc_scr[g0:g0 + DSA_HG * QB, :] = (acc_scr[g0:g0 + DSA_HG * QB, :]
                                               * jnp.concatenate(alphas, axis=0) + pv)
        return carry

    lax.fori_loop(0, n_kt, att_body, 0)

    for g in range(H // 2):
        parts = []
        for hh in (2 * g, 2 * g + 1):
            inv = 1.0 / l_scr[hh * QB:(hh + 1) * QB, :]
            ol = acc_scr[hh * QB:(hh + 1) * QB, :] * _rep(inv, C_KV_RANK // LANES, axis=1)
            parts.append(jnp.dot(ol.astype(jnp.bfloat16), wuvp_ref[hh], preferred_element_type=f32))
        o_ref[0, :, g * LANES:(g + 1) * LANES] = parts[0] + parts[1]


def _dsa_proj_kernel(x_ref, wq_ref, wkv_ref, wkvT_ref, wqi_ref, wkiT_ref, wwi_ref, qn_ref, kvn_ref, kvnT_ref,
                     cq_ref, ckv_ref, ckvT_ref, qidx_ref, kidxT_ref, widx_ref):
    f32 = jnp.float32
    bf = jnp.bfloat16
    xb = x_ref[...].astype(bf)

    def rms(z, g):
        return z * lax.rsqrt(jnp.mean(z * z, axis=1, keepdims=True) + 1e-6) * g

    cq_ref[...] = rms(jnp.dot(xb, wq_ref[...], preferred_element_type=f32), qn_ref[...]).astype(bf)
    ckv = rms(jnp.dot(xb, wkv_ref[...], preferred_element_type=f32), kvn_ref[...])
    ckv_ref[0] = ckv.astype(bf)
    nt = (((1,), (1,)), ((), ()))
    zT = lax.dot_general(wkvT_ref[...], xb, nt, preferred_element_type=f32)
    ssq = jnp.sum(zT * zT, axis=0, keepdims=True) * (1.0 / C_KV_RANK)
    ckvT_ref[0] = (zT * lax.rsqrt(ssq + 1e-6) * kvnT_ref[...]).astype(bf)
    zqi = jnp.dot(xb, wqi_ref[...], preferred_element_type=f32)
    for h in range(IDX_HEADS):
        qidx_ref[0, h] = zqi[:, h * IDX_DIM:(h + 1) * IDX_DIM].astype(bf)
    kidxT_ref[0] = lax.dot_general(wkiT_ref[...], xb, nt, preferred_element_type=f32).astype(bf)
    widx_ref[...] = jnp.dot(xb, wwi_ref[...], preferred_element_type=f32)[:, 0:IDX_HEADS]


def _dsa_proj(x, w_in, q_norm, kv_norm):
    B, T, D = x.shape
    N = B * T
    TK = DSA_TK
    NT = T // TK
    bf = jnp.bfloat16
    s1 = C_Q_RANK
    s2 = s1 + C_KV_RANK
    s3 = s2 + IDX_HEADS * IDX_DIM
    s4 = s3 + IDX_DIM
    wq = w_in[:, :s1].astype(bf)
    wkv = w_in[:, s1:s2].astype(bf)
    wqi = w_in[:, s2:s3].astype(bf)
    wkvT = wkv.T
    wkiT = w_in[:, s3:s4].T.astype(bf)
    wwi = jnp.pad(w_in[:, s4:], ((0, 0), (0, LANES - IDX_HEADS))).astype(bf)
    cs = lambda shape: pl.BlockSpec(shape, lambda i: (0,) * len(shape))
    cq, ckv, ckvT, qidx, kidxT, widx = pl.pallas_call(
        _dsa_proj_kernel,
        grid=(N // TK,),
        in_specs=[pl.BlockSpec((TK, D), lambda i: (i, 0)),
                  cs(wq.shape), cs(wkv.shape), cs(wkvT.shape), cs(wqi.shape), cs(wkiT.shape), cs(wwi.shape),
                  cs((1, C_Q_RANK)), cs((1, C_KV_RANK)), cs((C_KV_RANK, 1))],
        out_specs=[pl.BlockSpec((TK, C_Q_RANK), lambda i: (i, 0)),
                   pl.BlockSpec((1, TK, C_KV_RANK), lambda i: (i, 0, 0)),
                   pl.BlockSpec((1, C_KV_RANK, TK), lambda i: (i, 0, 0)),
                   pl.BlockSpec((1, IDX_HEADS, TK, IDX_DIM), lambda i: (i // NT, 0, i % NT, 0)),
                   pl.BlockSpec((1, IDX_DIM, TK), lambda i: (i, 0, 0)),
                   pl.BlockSpec((TK, IDX_HEADS), lambda i: (i, 0))],
        out_shape=[jax.ShapeDtypeStruct((N, C_Q_RANK), bf),
                   jax.ShapeDtypeStruct((B * NT, TK, C_KV_RANK), bf),
                   jax.ShapeDtypeStruct((B * NT, C_KV_RANK, TK), bf),
                   jax.ShapeDtypeStruct((B, IDX_HEADS, T, IDX_DIM), bf),
                   jax.ShapeDtypeStruct((B * NT, IDX_DIM, TK), bf),
                   jax.ShapeDtypeStruct((N, IDX_HEADS), jnp.float32)],
        compiler_params=pltpu.CompilerParams(dimension_semantics=("arbitrary",),
                                             vmem_limit_bytes=VMEM_LIMIT),
        name="dsa_proj",
    )(x.reshape(N, D), wq, wkv, wkvT, wqi, wkiT, wwi, q_norm[None, :], kv_norm[None, :], kv_norm[:, None])
    return (cq.reshape(B, T, C_Q_RANK), qidx, widx.reshape(B, T, IDX_HEADS),
            kidxT.reshape(B, NT, IDX_DIM, TK), ckvT.reshape(B, NT, C_KV_RANK, TK),
            ckv.reshape(B, NT, TK, C_KV_RANK))


def _dsa_attention(cq, qidx, w_idx, kidxT, ckvT, ckv, w_uq, w_uk, w_uv):
    B, T, _ = cq.shape
    topk = min(IDX_TOPK_MAX, T // 4)
    TK = DSA_TK
    NT = T // TK
    assert T % (2 * TK) == 0, "counting passes walk key tiles in pairs"
    H = C_HEADS
    bf = jnp.bfloat16
    slopes2 = jnp.exp2(-8.0 * jnp.arange(1, H + 1, dtype=jnp.float32) / H) * LOG2E
    wuq = w_uq.reshape(C_Q_RANK, H, C_HEAD_DIM).transpose(1, 0, 2).astype(bf)
    wukT = w_uk.transpose(1, 2, 0).astype(bf)
    wuv = w_uv.transpose(1, 0, 2)
    zeros = jnp.zeros_like(wuv)
    even = (jnp.arange(H) % 2 == 0)[:, None, None]
    wuvp = jnp.where(even, jnp.concatenate([wuv, zeros], -1), jnp.concatenate([zeros, wuv], -1)).astype(bf)

    kern = functools.partial(_dsa_kernel, topk=topk, seq_len=T)
    full = lambda shape: pl.BlockSpec(shape, lambda b, q, *_: (0,) * len(shape))
    grid_spec = pltpu.PrefetchScalarGridSpec(
        num_scalar_prefetch=0,
        grid=(B, T // QB),
        in_specs=[
            pl.BlockSpec(memory_space=pltpu.SMEM),
            pl.BlockSpec((1, QB, C_Q_RANK), lambda b, q: (b, q, 0)),
            pl.BlockSpec((1, IDX_HEADS, QB, IDX_DIM), lambda b, q: (b, 0, q, 0)),
            pl.BlockSpec((1, QB, IDX_HEADS), lambda b, q: (b, q, 0)),
            pl.BlockSpec((1, NT, IDX_DIM, TK), lambda b, q: (b, 0, 0, 0)),
            pl.BlockSpec((1, NT, C_KV_RANK, TK), lambda b, q: (b, 0, 0, 0)),
            pl.BlockSpec((1, NT, TK, C_KV_RANK), lambda b, q: (b, 0, 0, 0)),
            full((H, C_Q_RANK, C_HEAD_DIM)),
            full((H, C_HEAD_DIM, C_KV_RANK)),
            full((H, C_KV_RANK, LANES)),
        ],
        out_specs=pl.BlockSpec((1, QB, H * C_HEAD_DIM), lambda b, q: (b, q, 0)),
        scratch_shapes=[
            pltpu.VMEM((NT, QB, TK), jnp.int32),
            pltpu.VMEM((IDX_HEADS, QB, LANES), jnp.float32),
            pltpu.VMEM((H * QB, C_KV_RANK), bf),
            pltpu.VMEM((QB, TK), jnp.float32),
            pltpu.VMEM((H * QB, LANES), jnp.float32),
            pltpu.VMEM((H * QB, LANES), jnp.float32),
            pltpu.VMEM((H * QB, C_KV_RANK), jnp.float32),
            pltpu.VMEM((QB, LANES), jnp.int32),
            pltpu.VMEM((QB, LANES), jnp.int32),
        ],
    )
    return pl.pallas_call(
        kern,
        grid_spec=grid_spec,
        out_shape=jax.ShapeDtypeStruct((B, T, H * C_HEAD_DIM), jnp.float32),
        compiler_params=pltpu.CompilerParams(dimension_semantics=("arbitrary", "arbitrary"),
                                             vmem_limit_bytes=56 * 1024 * 1024),
        name="dsa_attention",
    )(slopes2, cq, qidx, w_idx, kidxT, ckvT, ckv, wuq, wukT, wuvp)


def _mixer_c(x, w_in, q_norm, kv_norm, w_uq, w_uk, w_uv):
    return _dsa_attention(*_dsa_proj(x, w_in, q_norm, kv_norm), w_uq, w_uk, w_uv)


def _moe_layer(layer, x1, route, cnt, routeT, p, w_gu, w_down, ln_g, ln_b, w_gate, w_proj):
    N = x1.shape[0]
    M = N * TOPK_IN_GROUP
    flat_e = routeT[0:TOPK_IN_GROUP].astype(jnp.int32).T.reshape(M)
    rank = routeT[4:4 + TOPK_IN_GROUP].astype(jnp.int32).T.reshape(M)
    counts = cnt[0, N_GROUPS:N_GROUPS + N_EXPERTS].astype(jnp.int32)
    return _moe_experts(layer, x1, flat_e, rank, counts, route, p, w_gu, w_down, ln_g, ln_b, w_gate, w_proj)


MOE_BLK = 512
MOE_TM = 256
MOE_DMA_UNROLL = 8


def _dispatch_kernel(dest_ref, x_ref, xs_init_ref, xs_ref, sem):
    del xs_init_ref
    base = pl.program_id(0) * (TOPK_IN_GROUP * MOE_TM)

    def row_copy(t, k):
        d = dest_ref[base + TOPK_IN_GROUP * t + k]
        return pltpu.make_async_copy(x_ref.at[pl.ds(t, 1), :], xs_ref.at[pl.ds(d, 1), :], sem)

    def issue(t, c):
        for k in range(TOPK_IN_GROUP):
            row_copy(t, k).start(priority=k)
        return c

    def drain(t, c):
        for k in range(TOPK_IN_GROUP):
            row_copy(t, k).wait()
        return c

    lax.fori_loop(0, MOE_TM, issue, 0, unroll=MOE_DMA_UNROLL)
    lax.fori_loop(0, MOE_TM, drain, 0, unroll=MOE_DMA_UNROLL)


def _expert_kernel(be_ref, nu_ref, xs_ref, wgu_ref, wd_ref, ys_ref, wgu_bf, wd_bf):
    b = pl.program_id(0)
    F = wd_bf.shape[0]

    @pl.when(b < nu_ref[0])
    def _():
        e = be_ref[b]
        prev = be_ref[jnp.maximum(b - 1, 0)]

        @pl.when((b == 0) | (e != prev))
        def _():
            wgu_bf[...] = wgu_ref[0, 0].astype(jnp.bfloat16)
            wd_bf[...] = wd_ref[0, 0].astype(jnp.bfloat16)

        h = jnp.dot(xs_ref[...].astype(jnp.bfloat16), wgu_bf[...], preferred_element_type=jnp.float32)
        a = h[:, :F]
        act = a * jax.nn.sigmoid(a) * h[:, F:]
        ys_ref[...] = jnp.dot(act.astype(jnp.bfloat16), wd_bf[...], preferred_element_type=jnp.float32)

    @pl.when(b >= nu_ref[0])
    def _():
        ys_ref[...] = jnp.zeros(ys_ref.shape, ys_ref.dtype)


def _combine_kernel(dest_ref, ys_ref, route_ref, x1_ref, p_ref, lng_ref, lnb_ref, wg_ref, wp_ref,
                    o_ref, buf, sem):
    i = pl.program_id(0)
    n = pl.num_programs(0)

    def row_copy(step, slot, t, k):
        d = dest_ref[step * (TOPK_IN_GROUP * MOE_TM) + TOPK_IN_GROUP * t + k]
        return pltpu.make_async_copy(ys_ref.at[pl.ds(d, 1), :], buf.at[slot, k, pl.ds(t, 1), :], sem.at[slot])

    def issue(step, slot):
        def body(t, c):
            for k in range(TOPK_IN_GROUP):
                row_copy(step, slot, t, k).start(priority=k)
            return c
        lax.fori_loop(0, MOE_TM, body, 0, unroll=MOE_DMA_UNROLL)

    def drain(step, slot):
        def body(t, c):
            for k in range(TOPK_IN_GROUP):
                row_copy(step, slot, t, k).wait()
            return c
        lax.fori_loop(0, MOE_TM, body, 0, unroll=MOE_DMA_UNROLL)

    @pl.when(i == 0)
    def _():
        issue(0, 0)

    @pl.when(i + 1 < n)
    def _():
        issue(i + 1, (i + 1) % 2)

    slot = i % 2
    drain(i, slot)
    g = route_ref[...]
    m = g[:, 2:3] * buf[slot, 0] + g[:, 3:4] * buf[slot, 1]
    x2 = _ln_rows(ALPHA * x1_ref[...] + m, lng_ref[...], lnb_ref[...])
    o_ref[...] = x2 + jax.nn.sigmoid(_bdot(x2, wg_ref[...])) * _bdot(p_ref[...], wp_ref[...])


def _moe_experts(layer, xt, flat_e, rank, counts, route, p, w_gu, w_down, ln_g, ln_b, w_gate, w_proj):
    N, D = xt.shape
    M = flat_e.shape[0]
    F = w_down.shape[2]
    BLK, TM = MOE_BLK, MOE_TM
    nb = -(-(M + N_EXPERTS * (BLK - 1)) // BLK)
    R = nb * BLK
    padded = ((counts + BLK - 1) // BLK) * BLK
    pend = jnp.cumsum(padded)
    pstart = pend - padded
    dest = (pstart[flat_e] + rank).astype(jnp.int32)
    nused = (pend[-1:] // BLK).astype(jnp.int32)
    blk_e = jnp.minimum(jnp.sum(pend[None, :] <= (jnp.arange(nb) * BLK)[:, None], axis=1),
                        N_EXPERTS - 1).astype(jnp.int32)
    vmem = 56 * 1024 * 1024

    xs = pl.pallas_call(
        _dispatch_kernel,
        grid_spec=pltpu.PrefetchScalarGridSpec(
            num_scalar_prefetch=1,
            grid=(N // TM,),
            in_specs=[pl.BlockSpec((TM, D), lambda i, d: (i, 0)),
                      pl.BlockSpec(memory_space=pl.ANY)],
            out_specs=pl.BlockSpec(memory_space=pl.ANY),
            scratch_shapes=[pltpu.SemaphoreType.DMA(())],
        ),
        out_shape=jax.ShapeDtypeStruct((R, D), jnp.float32),
        input_output_aliases={2: 0},
        compiler_params=pltpu.CompilerParams(dimension_semantics=("arbitrary",), vmem_limit_bytes=vmem),
        name="moe_dispatch",
    )(dest, xt, jnp.zeros((R, D), jnp.float32))

    def blk(b, be, nu):
        return jnp.minimum(b, nu[0] - 1)

    ys = pl.pallas_call(
        _expert_kernel,
        grid_spec=pltpu.PrefetchScalarGridSpec(
            num_scalar_prefetch=2,
            grid=(nb,),
            in_specs=[pl.BlockSpec((BLK, D), lambda b, be, nu: (blk(b, be, nu), 0)),
                      pl.BlockSpec((1, 1, D, 2 * F), lambda b, be, nu: (layer, be[blk(b, be, nu)], 0, 0)),
                      pl.BlockSpec((1, 1, F, D), lambda b, be, nu: (layer, be[blk(b, be, nu)], 0, 0))],
            out_specs=pl.BlockSpec((BLK, D), lambda b, be, nu: (b, 0)),
            scratch_shapes=[pltpu.VMEM((D, 2 * F), jnp.bfloat16), pltpu.VMEM((F, D), jnp.bfloat16)],
        ),
        out_shape=jax.ShapeDtypeStruct((R, D), jnp.float32),
        compiler_params=pltpu.CompilerParams(dimension_semantics=("arbitrary",), vmem_limit_bytes=vmem),
        name="moe_experts",
    )(blk_e, nused, xs, w_gu, w_down)

    return pl.pallas_call(
        _combine_kernel,
        grid_spec=pltpu.PrefetchScalarGridSpec(
            num_scalar_prefetch=1,
            grid=(N // TM,),
            in_specs=[pl.BlockSpec(memory_space=pl.ANY),
                      pl.BlockSpec((TM, ROUTE_W), lambda i, d: (i, 0)),
                      pl.BlockSpec((TM, D), lambda i, d: (i, 0)),
                      pl.BlockSpec((TM, p.shape[1]), lambda i, d: (i, 0)),
                      pl.BlockSpec((1, D), lambda i, d: (0, 0)),
                      pl.BlockSpec((1, D), lambda i, d: (0, 0)),
                      pl.BlockSpec((D, D), lambda i, d: (0, 0)),
                      pl.BlockSpec((p.shape[1], D), lambda i, d: (0, 0))],
            out_specs=pl.BlockSpec((TM, D), lambda i, d: (i, 0)),
            scratch_shapes=[pltpu.VMEM((2, TOPK_IN_GROUP, TM, D), jnp.float32),
                            pltpu.SemaphoreType.DMA((2,))],
        ),
        out_shape=jax.ShapeDtypeStruct((N, D), jnp.float32),
        compiler_params=pltpu.CompilerParams(dimension_semantics=("arbitrary",), vmem_limit_bytes=vmem),
        name="moe_combine",
    )(dest, ys, route, xt, p, ln_g[None, :], ln_b[None, :],
      w_gate.astype(jnp.bfloat16), w_proj.astype(jnp.bfloat16))


def kernel(x, p, ab_w_in, ret_gn_g, pool_w, pool_scale, ab_w_out, c_w_in, c_q_norm, c_kv_norm, c_w_uq, c_w_uk, c_w_uv, c_w_out, ln1_g, ln1_b, ln2_g, ln2_b, moe_w_group, moe_b_group, moe_w_expert, moe_b_expert, moe_w_gu, moe_w_down, ple_w_gate, ple_w_proj):
    B, T, D = x.shape
    N = B * T
    for i in range(DEPTH):
        j = i // 2
        wr, br = _router_weights(moe_w_group[i], moe_b_group[i], moe_w_expert[i], moe_b_expert[i])
        if i % 2 == 0:
            x1, route, cnt, routeT = _mixer_ab_layer(x, ab_w_in[j], ret_gn_g[j], pool_w[j], pool_scale[j],
                                                     ab_w_out[j], ln1_g[i], ln1_b[i], wr, br)
            x1, route = x1.reshape(N, D), route.reshape(N, ROUTE_W)
        else:
            o = _mixer_c(x, c_w_in[j], c_q_norm[j], c_kv_norm[j], c_w_uq[j], c_w_uk[j], c_w_uv[j])
            x1, route, cnt, routeT = _outproj_layer(o.reshape(N, -1), x.reshape(N, D), c_w_out[j],
                                                    ln1_g[i], ln1_b[i], wr, br)
        x = _moe_layer(i, x1, route, cnt, routeT, p[i].reshape(N, -1), moe_w_gu, moe_w_down,
                       ln2_g[i], ln2_b[i], ple_w_gate[i], ple_w_proj[i]).reshape(B, T, D)
    return x
```

```python
import functools
import math
import jax
import jax.numpy as jnp
from jax import lax
from jax.experimental import pallas as pl
from jax.experimental.pallas import tpu as pltpu

D_MODEL = 1024
DEPTH = 2
RET_HEADS = 4
RET_DK = 128
RET_DV = 128
RET_CHUNK = 128
POOL_WINDOWS = (2, 4, 8, 16)
POOL_GROUP = 128
C_HEADS = 16
C_HEAD_DIM = 64
C_Q_RANK = 256
C_KV_RANK = 256
IDX_HEADS = 8
IDX_DIM = 64
IDX_TOPK_MAX = 256
Q_BLOCK = 128
N_GROUPS = 4
EXPERTS_PER_GROUP = 8
N_EXPERTS = N_GROUPS * EXPERTS_PER_GROUP
TOPK_IN_GROUP = 2
MOE_BLOCK = 128
ALPHA = (2.0 * DEPTH) ** 0.25
LN_EPS = 1e-5


def _mm_kernel(x_ref, w_ref, o_ref):
    o_ref[...] = jnp.dot(x_ref[...].astype(jnp.bfloat16), w_ref[...].astype(jnp.bfloat16),
                         preferred_element_type=jnp.float32)


def _matmul(x, w, tm=512):
    M, K = x.shape
    N = w.shape[1]
    npad = (-N) % 128
    if npad:
        w = jnp.pad(w, ((0, 0), (0, npad)))
    Np = N + npad
    out = pl.pallas_call(
        _mm_kernel,
        grid=(M // tm,),
        in_specs=[pl.BlockSpec((tm, K), lambda i: (i, 0)),
                  pl.BlockSpec((K, Np), lambda i: (0, 0))],
        out_specs=pl.BlockSpec((tm, Np), lambda i: (i, 0)),
        out_shape=jax.ShapeDtypeStruct((M, Np), jnp.float32),
        compiler_params=pltpu.CompilerParams(dimension_semantics=("arbitrary",),
                                             vmem_limit_bytes=56 * 1024 * 1024),
    )(x, w)
    return out[:, :N] if npad else out


def _mm3(x, w):
    B, T, K = x.shape
    return _matmul(x.reshape(B * T, K), w).reshape(B, T, w.shape[1])


def _layer_norm(x, g, b):
    mu = x.mean(-1, keepdims=True)
    var = jnp.square(x - mu).mean(-1, keepdims=True)
    return (x - mu) * lax.rsqrt(var + LN_EPS) * g + b


def _rms_norm(x, g):
    return x * lax.rsqrt(jnp.mean(x * x, -1, keepdims=True) + 1e-6) * g


TOK_TM = 512
POOL_TAIL = 16
VMEM_LIMIT = 56 * 1024 * 1024
ROUTE_W = 128
ROUTE_ROWS = 8


def _bdot(a, b):
    return jnp.dot(a.astype(jnp.bfloat16), b.astype(jnp.bfloat16), preferred_element_type=jnp.float32)


def _ln_rows(v, g, b):
    mu = jnp.mean(v, axis=1, keepdims=True)
    d = v - mu
    var = jnp.mean(d * d, axis=1, keepdims=True)
    return d * lax.rsqrt(var + LN_EPS) * g + b


def _route_rows(x1, wr_ref, br_ref, tri_ref, run_scr, cnt_ref):
    f32 = jnp.float32
    logits = _bdot(x1, wr_ref[...]) + br_ref[...]
    lane = lax.broadcasted_iota(jnp.int32, logits.shape, 1)
    lane_f = lane.astype(f32)
    ninf = -jnp.inf
    gl = jnp.where(lane < N_GROUPS, logits, ninf)
    gmax = jnp.max(gl, axis=1, keepdims=True)
    grp = jnp.min(jnp.where(gl == gmax, lane_f, float(ROUTE_W)), axis=1, keepdims=True)
    g_w = 1.0 / jnp.sum(jnp.exp(gl - gmax), axis=1, keepdims=True)
    egrp = ((lane - N_GROUPS) >> 3).astype(f32)
    el = jnp.where(lane >= N_GROUPS, jnp.where(egrp == grp, logits, ninf), ninf)
    v1 = jnp.max(el, axis=1, keepdims=True)
    i1 = jnp.min(jnp.where(el == v1, lane_f, float(ROUTE_W)), axis=1, keepdims=True)
    el2 = jnp.where(lane_f == i1, ninf, el)
    v2 = jnp.max(el2, axis=1, keepdims=True)
    i2 = jnp.min(jnp.where(el2 == v2, lane_f, float(ROUTE_W)), axis=1, keepdims=True)
    e = jnp.exp(v2 - v1)
    den = 1.0 / (1.0 + e)
    oh1 = jnp.where(lane_f == i1, 1.0, 0.0)
    oh2 = jnp.where(lane_f == i2, 1.0, 0.0)
    oh = oh1 + oh2
    base = run_scr[0:1, :] + jnp.dot(tri_ref[...], oh.astype(jnp.bfloat16), preferred_element_type=f32)
    r1 = jnp.sum(oh1 * base, axis=1, keepdims=True)
    r2 = jnp.sum(oh2 * base, axis=1, keepdims=True)
    run_new = run_scr[0:1, :] + jnp.sum(oh, axis=0, keepdims=True)
    run_scr[0:1, :] = run_new
    cnt_ref[...] = jnp.broadcast_to(run_new, cnt_ref.shape)
    sel = lambda l, v, rest: jnp.where(lane == l, v, rest)
    return sel(0, i1 - N_GROUPS, sel(1, i2 - N_GROUPS, sel(2, g_w * den, sel(3, g_w * e * den,
                                                                             sel(4, r1, sel(5, r2, 0.0))))))


def _mixer_ab_kernel(cdec_ref, x_ref, win_ref, idec_ref, qdec_ref, kdec_ref, gng_ref, bandc_ref, bandp_ref,
                     pw_ref, psc_ref, wout_ref, lng_ref, lnb_ref, wr_ref, br_ref, tri_ref,
                     x1_ref, route_ref, cnt_ref, routeT_ref, r_scr, tail_scr, mixed_scr, run_scr):
    f32 = jnp.float32
    bf = jnp.bfloat16
    C = RET_CHUNK
    TM = x_ref.shape[1]
    nq = RET_HEADS * RET_DK
    nv = RET_HEADS * RET_DV
    ti = pl.program_id(1)

    @pl.when(ti == 0)
    def _():
        r_scr[...] = jnp.zeros(r_scr.shape, f32)
        tail_scr[...] = jnp.zeros(tail_scr.shape, f32)

    x = x_ref[0]
    xb = x.astype(bf)
    zq = jnp.dot(xb, win_ref[:, 0:nq], preferred_element_type=f32)
    zk = jnp.dot(xb, win_ref[:, nq:2 * nq], preferred_element_type=f32) * (RET_DK ** -0.5)
    zv = jnp.dot(xb, win_ref[:, 2 * nq:2 * nq + nv], preferred_element_type=f32)
    zg = jnp.dot(xb, win_ref[:, 2 * nq + nv:2 * nq + 2 * nv], preferred_element_type=f32)
    zu = jnp.dot(xb, win_ref[:, 2 * nq + 2 * nv:], preferred_element_type=f32)

    for h in range(RET_HEADS):
        hs = slice(h * RET_DK, (h + 1) * RET_DK)
        R = r_scr[h]
        for c in range(TM // C):
            cs = slice(c * C, (c + 1) * C)
            qc, kc, vc = zq[cs, hs], zk[cs, hs], zv[cs, hs]
            s = lax.dot_general(qc.astype(bf), kc.astype(bf), (((1,), (1,)), ((), ())),
                                preferred_element_type=f32) * idec_ref[h]
            o = _bdot(s, vc) + _bdot(qc * qdec_ref[h], R)
            R = R * cdec_ref[h] + lax.dot_general((kc * kdec_ref[h]).astype(bf), vc.astype(bf),
                                                  (((0,), (0,)), ((), ())), preferred_element_type=f32)
            mu = jnp.mean(o, axis=1, keepdims=True)
            d = o - mu
            var = jnp.mean(d * d, axis=1, keepdims=True)
            gate = zg[cs, hs]
            ret = d * lax.rsqrt(var + LN_EPS) * gng_ref[:, hs] * (gate * jax.nn.sigmoid(gate))
            mixed_scr[cs, hs] = ret.astype(bf)
        r_scr[h] = R

    t_glob = ti * TM + lax.broadcasted_iota(jnp.int32, (TM, POOL_GROUP), 0)
    tail = tail_scr[...]
    for gi, w in enumerate(POOL_WINDOWS):
        gs = slice(gi * POOL_GROUP, (gi + 1) * POOL_GROUP)
        u = zu[:, gs]
        u_hi = u.astype(bf)
        u_lo = (u - u_hi.astype(f32)).astype(bf)
        tl = tail[:, gs]
        t_hi = tl.astype(bf)
        t_lo = (tl - t_hi.astype(f32)).astype(bf)
        wsum = (jnp.dot(bandc_ref[gi], u_hi, preferred_element_type=f32)
                + jnp.dot(bandc_ref[gi], u_lo, preferred_element_type=f32)
                + jnp.dot(bandp_ref[gi], t_hi, preferred_element_type=f32)
                + jnp.dot(bandp_ref[gi], t_lo, preferred_element_type=f32))
        cnt = jnp.minimum(t_glob + 1, w).astype(f32)
        dpool = wsum / cnt - u
        y = _bdot(dpool, pw_ref[gi]) * psc_ref[:, gs]
        mixed_scr[:, nv + gi * POOL_GROUP:nv + (gi + 1) * POOL_GROUP] = y.astype(bf)
    tail_scr[...] = zu[TM - POOL_TAIL:, :]

    hmix = jnp.dot(mixed_scr[...], wout_ref[...], preferred_element_type=f32)
    x1 = _ln_rows(ALPHA * x + hmix, lng_ref[...], lnb_ref[...])
    x1_ref[0] = x1
    @pl.when((pl.program_id(0) == 0) & (ti == 0))
    def _():
        run_scr[...] = jnp.zeros(run_scr.shape, f32)

    route = _route_rows(x1, wr_ref, br_ref, tri_ref, run_scr, cnt_ref)
    route_ref[0] = route
    routeT_ref[...] = route.T[0:ROUTE_ROWS, :]


def _const_spec(shape):
    return pl.BlockSpec(shape, lambda b, t: (0,) * len(shape))


def _router_weights(w_group, b_group, w_expert, b_expert):
    D = w_group.shape[0]
    pad = ROUTE_W - N_GROUPS - N_EXPERTS
    wr = jnp.concatenate([w_group, w_expert, jnp.zeros((D, pad), jnp.float32)], axis=1).astype(jnp.bfloat16)
    br = jnp.concatenate([b_group, b_expert, jnp.zeros((pad,), jnp.float32)])[None, :]
    return wr, br


def _mixer_ab_layer(x, w_in, gn_g, pool_w, pool_scale, w_out, ln_g, ln_b, wr, br):
    B, T, D = x.shape
    TM = TOK_TM
    C = RET_CHUNK
    H = RET_HEADS
    bf = jnp.bfloat16
    log_g = jnp.log1p(-jnp.exp2(-5.0 - jnp.arange(H, dtype=jnp.float32)))
    i = jnp.arange(C, dtype=jnp.float32)
    diff = i[:, None] - i[None, :]
    idec = jnp.where(diff >= 0, jnp.exp(log_g[:, None, None] * jnp.maximum(diff, 0.0)), 0.0)
    qdec = jnp.broadcast_to(jnp.exp(log_g[:, None] * (i + 1.0))[:, :, None], (H, C, RET_DK))
    kdec = jnp.broadcast_to(jnp.exp(log_g[:, None] * (C - 1.0 - i))[:, :, None], (H, C, RET_DK))
    cdec = jnp.exp(log_g * C)
    r = jnp.arange(TM)[:, None]
    bandc = jnp.stack([((r - jnp.arange(TM)[None, :] >= 0) & (r - jnp.arange(TM)[None, :] < w))
                       for w in POOL_WINDOWS]).astype(bf)
    jp = jnp.arange(POOL_TAIL)[None, :] - POOL_TAIL
    bandp = jnp.stack([((r - jp) < w) for w in POOL_WINDOWS]).astype(bf)
    nG = len(POOL_WINDOWS)
    return pl.pallas_call(
        _mixer_ab_kernel,
        grid=(B, T // TM),
        in_specs=[pl.BlockSpec(memory_space=pltpu.SMEM),
                  pl.BlockSpec((1, TM, D), lambda b, t: (b, t, 0)),
                  _const_spec(w_in.shape), _const_spec((H, C, C)), _const_spec((H, C, RET_DK)),
                  _const_spec((H, C, RET_DK)), _const_spec((1, H * RET_DV)),
                  _const_spec((nG, TM, TM)), _const_spec((nG, TM, POOL_TAIL)),
                  _const_spec((nG, POOL_GROUP, POOL_GROUP)), _const_spec((1, nG * POOL_GROUP)),
                  _const_spec(w_out.shape), _const_spec((1, D)), _const_spec((1, D)),
                  _const_spec((D, ROUTE_W)), _const_spec((1, ROUTE_W)), _const_spec((TM, TM))],
        out_specs=[pl.BlockSpec((1, TM, D), lambda b, t: (b, t, 0)),
                   pl.BlockSpec((1, TM, ROUTE_W), lambda b, t: (b, t, 0)),
                   _const_spec((8, ROUTE_W)),
                   pl.BlockSpec((ROUTE_ROWS, TM), lambda b, t: (0, b * (T // TM) + t))],
        out_shape=[jax.ShapeDtypeStruct((B, T, D), jnp.float32),
                   jax.ShapeDtypeStruct((B, T, ROUTE_W), jnp.float32),
                   jax.ShapeDtypeStruct((8, ROUTE_W), jnp.float32),
                   jax.ShapeDtypeStruct((ROUTE_ROWS, B * T), jnp.float32)],
        scratch_shapes=[pltpu.VMEM((H, RET_DK, RET_DV), jnp.float32),
                        pltpu.VMEM((POOL_TAIL, nG * POOL_GROUP), jnp.float32),
                        pltpu.VMEM((TM, w_out.shape[0]), bf),
                        pltpu.VMEM((8, ROUTE_W), jnp.float32)],
        compiler_params=pltpu.CompilerParams(dimension_semantics=("arbitrary", "arbitrary"),
                                             vmem_limit_bytes=VMEM_LIMIT),
        name="mixer_ab_layer",
    )(cdec, x, w_in.astype(bf), idec, qdec, kdec, gn_g[None, :], bandc, bandp,
      pool_w.astype(bf), pool_scale[None, :], w_out.astype(bf), ln_g[None, :], ln_b[None, :], wr, br,
      _strict_lower_tri(TM))


def _outproj_kernel(o_ref, x_ref, wout_ref, lng_ref, lnb_ref, wr_ref, br_ref, tri_ref,
                    x1_ref, route_ref, cnt_ref, routeT_ref, run_scr):
    @pl.when(pl.program_id(0) == 0)
    def _():
        run_scr[...] = jnp.zeros(run_scr.shape, jnp.float32)

    h = _bdot(o_ref[...], wout_ref[...])
    x1 = _ln_rows(ALPHA * x_ref[...] + h, lng_ref[...], lnb_ref[...])
    x1_ref[...] = x1
    route = _route_rows(x1, wr_ref, br_ref, tri_ref, run_scr, cnt_ref)
    route_ref[...] = route
    routeT_ref[...] = route.T[0:ROUTE_ROWS, :]


def _strict_lower_tri(n):
    r = jnp.arange(n)
    return (r[:, None] > r[None, :]).astype(jnp.bfloat16)


def _outproj_layer(o, x, w_out, ln_g, ln_b, wr, br):
    N, D = x.shape
    TM = TOK_TM
    cs = lambda shape: pl.BlockSpec(shape, lambda i: (0,) * len(shape))
    return pl.pallas_call(
        _outproj_kernel,
        grid=(N // TM,),
        in_specs=[pl.BlockSpec((TM, o.shape[1]), lambda i: (i, 0)),
                  pl.BlockSpec((TM, D), lambda i: (i, 0)),
                  cs(w_out.shape), cs((1, D)), cs((1, D)), cs((D, ROUTE_W)), cs((1, ROUTE_W)),
                  cs((TM, TM))],
        out_specs=[pl.BlockSpec((TM, D), lambda i: (i, 0)),
                   pl.BlockSpec((TM, ROUTE_W), lambda i: (i, 0)),
                   cs((8, ROUTE_W)),
                   pl.BlockSpec((ROUTE_ROWS, TM), lambda i: (0, i))],
        out_shape=[jax.ShapeDtypeStruct((N, D), jnp.float32),
                   jax.ShapeDtypeStruct((N, ROUTE_W), jnp.float32),
                   jax.ShapeDtypeStruct((8, ROUTE_W), jnp.float32),
                   jax.ShapeDtypeStruct((ROUTE_ROWS, N), jnp.float32)],
        scratch_shapes=[pltpu.VMEM((8, ROUTE_W), jnp.float32)],
        compiler_params=pltpu.CompilerParams(dimension_semantics=("arbitrary",),
                                             vmem_limit_bytes=VMEM_LIMIT),
        name="outproj_layer",
    )(o, x, w_out.astype(jnp.bfloat16), ln_g[None, :], ln_b[None, :], wr, br, _strict_lower_tri(TM))


LANES = 128
QB = Q_BLOCK
DSA_TK = 512
DSA_HG = 4
DSA_VALUE_STEPS = 24
DSA_ROUNDS_PER_TEST = 4
DSA_COARSE_STEPS = 16
NEG_BIG = -1e30
KEY_LO0 = -2139095040
KEY_HI0 = 2139095040
LOG2E = 1.4426950408889634
BIG_IDX = 1 << 30


def _rep(x, n, axis):
    return jnp.concatenate([x] * n, axis=axis) if n > 1 else x


def _sum_lane_chunks(v, width):
    out = v[:, 0:LANES]
    for c in range(1, width // LANES):
        out = out + v[:, c * LANES:(c + 1) * LANES]
    return out


def _dsa_kernel(slopes_ref, cq_ref, qidx_ref, widx_ref, kidxT_ref, ckvT_ref, ckv_ref,
                wuq_ref, wukT_ref, wuvp_ref, o_ref,
                keys_scr, wb_scr, q_scr, a_scr, m_scr, l_scr, acc_scr,
                th_scr, c_scr, *, topk, seq_len):
    f32 = jnp.float32
    TK = DSA_TK
    NCH = TK // LANES
    H = C_HEADS
    qb = pl.program_id(1)
    t0 = qb * QB
    n_kt = (t0 + QB - 1) // TK + 1
    row = lax.broadcasted_iota(jnp.int32, (QB, TK), 0)
    col = lax.broadcasted_iota(jnp.int32, (QB, TK), 1)
    kf = float(topk)

    for h in range(IDX_HEADS):
        wb_scr[h] = jnp.broadcast_to(widx_ref[0, :, h:h + 1], (QB, LANES))

    def idx_body(j, carry):
        kT = kidxT_ref[0, j]
        s_all = jnp.dot(qidx_ref[0].reshape(IDX_HEADS * QB, IDX_DIM), kT, preferred_element_type=f32)
        acc = jnp.zeros((QB, TK), f32)
        for h in range(IDX_HEADS):
            acc = acc + _rep(wb_scr[h], NCH, axis=1) * jnp.maximum(s_all[h * QB:(h + 1) * QB, :], 0.0)
        causal = (j * TK + col) <= (t0 + row)
        sc = jnp.where(causal, acc, -jnp.inf)
        b = pltpu.bitcast(sc, jnp.int32)
        keys_scr[j] = b ^ ((b >> 31) & 0x7FFFFFFF)
        vmax, vmin = carry
        sc_hi = jnp.where(causal, acc, jnp.inf)
        for c in range(NCH):
            vmax = jnp.maximum(vmax, sc[:, c * LANES:(c + 1) * LANES])
            vmin = jnp.minimum(vmin, sc_hi[:, c * LANES:(c + 1) * LANES])
        return vmax, vmin

    vmax, vmin = lax.fori_loop(0, n_kt, idx_body, (jnp.full((QB, LANES), -jnp.inf, f32),
                                                   jnp.full((QB, LANES), jnp.inf, f32)))

    def to_key(v):
        b = pltpu.bitcast(v, jnp.int32)
        return b ^ ((b >> 31) & 0x7FFFFFFF)

    def from_key(k):
        return pltpu.bitcast(k ^ ((k >> 31) & 0x7FFFFFFF), f32)

    @pl.when(n_kt < keys_scr.shape[0])
    def _():
        keys_scr[n_kt] = jnp.full((QB, TK), KEY_LO0 - 1, jnp.int32)

    def count_where(ones_fn):
        def body(j2, c):
            j = 2 * j2
            return (c + _sum_lane_chunks(ones_fn(j, keys_scr[j]), TK)
                    + _sum_lane_chunks(ones_fn(j + 1, keys_scr[j + 1]), TK))
        c = lax.fori_loop(0, (n_kt + 1) // 2, body, jnp.zeros((QB, LANES), f32))
        return jnp.broadcast_to(jnp.sum(c, axis=1, keepdims=True), (QB, LANES))

    n_valid = (t0 + 1 + lax.broadcasted_iota(jnp.int32, (QB, LANES), 0)).astype(f32)
    small = n_valid <= kf
    kmax = to_key(jnp.broadcast_to(jnp.max(vmax, axis=1, keepdims=True), (QB, LANES)))
    kmin = to_key(jnp.broadcast_to(jnp.min(vmin, axis=1, keepdims=True), (QB, LANES)))
    lo0 = jnp.where(small, KEY_LO0, kmin)
    hi0 = jnp.where(small, KEY_LO0 + 1, kmax + 1)

    def row_done(lo, hi, clo):
        return jnp.where(clo == kf, 1.0, jnp.where(hi == lo + 1, 1.0, 0.0))

    def search_round(mid_fn, lo, hi, clo, chi):
        mid = jnp.minimum(jnp.maximum(mid_fn(lo, hi), lo + 1), hi - 1)
        midb = _rep(mid, NCH, axis=1)
        cnt = count_where(lambda j, k: jnp.where(k >= midb, 1.0, 0.0))
        ge = jnp.where(small, kf, cnt) >= kf
        return jnp.where(ge, mid, lo), jnp.where(ge, hi, mid), jnp.where(ge, cnt, clo), jnp.where(ge, chi, cnt)

    def search(mid_fn, rounds_per_test, max_rounds, st):
        def cond(s):
            return (s[1] > 0) & (s[0] < max_rounds)

        def body(s):
            it, _, lo, hi, clo, chi = s
            for _ in range(rounds_per_test):
                lo, hi, clo, chi = search_round(mid_fn, lo, hi, clo, chi)
            pending = (jnp.min(row_done(lo, hi, clo)) < 0.5).astype(jnp.int32)
            return it + rounds_per_test, pending, lo, hi, clo, chi

        return lax.while_loop(cond, body, st)

    value_mid = lambda lo, hi: to_key(0.5 * from_key(lo) + 0.5 * from_key(hi - 1))
    key_mid = lambda lo, hi: (lo & hi) + ((lo ^ hi) >> 1)
    pending0 = (jnp.min(row_done(lo0, hi0, n_valid)) < 0.5).astype(jnp.int32)
    st = (jnp.int32(0), pending0, lo0, hi0, n_valid, jnp.zeros((QB, LANES), f32))
    st = search(value_mid, DSA_ROUNDS_PER_TEST, DSA_COARSE_STEPS, st)
    st = search(value_mid, DSA_ROUNDS_PER_TEST // 2, DSA_VALUE_STEPS, st)
    st = search(key_mid, DSA_ROUNDS_PER_TEST, 32 + DSA_ROUNDS_PER_TEST, (jnp.int32(0),) + st[1:])
    _, _, theta, _, n_ge, n_gt = st
    th_scr[...] = theta
    thb = _rep(theta, NCH, axis=1)

    need = kf - n_gt
    excess = n_ge > kf
    c_scr[...] = jnp.full((QB, LANES), BIG_IDX, jnp.int32)

    @pl.when(jnp.max(jnp.where(excess, 1.0, 0.0)) > 0.0)
    def _():
        def tie_body(i, carry):
            lo, hi = carry
            mid = (lo + hi) >> 1
            midb = _rep(mid, NCH, axis=1)
            cnt = count_where(lambda j, k: jnp.where(
                k == thb, jnp.where((j * TK + col) <= midb, 1.0, 0.0), 0.0))
            ok = cnt >= need
            return jnp.where(ok, lo, mid + 1), jnp.where(ok, mid, hi)
        nbits = max(1, (seq_len - 1).bit_length())
        lo, _ = lax.fori_loop(0, nbits, tie_body,
                              (jnp.zeros((QB, LANES), jnp.int32),
                               jnp.full((QB, LANES), seq_len - 1, jnp.int32)))
        c_scr[...] = jnp.where(excess, lo, BIG_IDX)

    cq = cq_ref[0]
    for h in range(H):
        qh = jnp.dot(cq, wuq_ref[h], preferred_element_type=f32).astype(jnp.bfloat16)
        qa = jnp.dot(qh, wukT_ref[h], preferred_element_type=f32) * (C_HEAD_DIM ** -0.5 * LOG2E)
        q_scr[h * QB:(h + 1) * QB, :] = qa.astype(jnp.bfloat16)

    m_scr[...] = jnp.full(m_scr.shape, NEG_BIG, f32)
    l_scr[...] = jnp.zeros(l_scr.shape, f32)
    acc_scr[...] = jnp.zeros(acc_scr.shape, f32)

    def att_body(j, carry):
        kT = ckvT_ref[0, j]
        kv = ckv_ref[0, j]
        k = keys_scr[j]
        thb_ = _rep(th_scr[...], NCH, axis=1)
        cb_ = _rep(c_scr[...], NCH, axis=1)
        spos = j * TK + col
        keep_tie = jnp.where(spos <= cb_, 0.0, NEG_BIG)
        a_scr[...] = jnp.where(k > thb_, 0.0, jnp.where(k == thb_, keep_tie, NEG_BIG))
        srel8 = (j * TK - t0 + lax.broadcasted_iota(jnp.int32, (8, TK), 1)).astype(f32)

        for g in range(H // DSA_HG):
            g0 = g * DSA_HG * QB
            lg = jnp.dot(q_scr[g0:g0 + DSA_HG * QB, :], kT, preferred_element_type=f32)
            ps, alphas = [], []
            for u in range(DSA_HG):
                h = g * DSA_HG + u
                r = slice(h * QB, (h + 1) * QB)
                bh = _rep(slopes_ref[h] * srel8, QB // 8, axis=0)
                l = lg[u * QB:(u + 1) * QB, :] + a_scr[...] + bh
                m_old = m_scr[r, :]
                m_new = jnp.maximum(m_old, jnp.max(l, axis=1, keepdims=True))
                p = jnp.exp2(l - _rep(m_new, NCH, axis=1))
                alpha = jnp.exp2(m_old - m_new)
                rs = jnp.sum(_sum_lane_chunks(p, TK), axis=1, keepdims=True)
                l_scr[r, :] = alpha * l_scr[r, :] + rs
                m_scr[r, :] = m_new
                ps.append(p.astype(jnp.bfloat16))
                alphas.append(_rep(alpha, C_KV_RANK // LANES, axis=1))
            pv = jnp.dot(jnp.concatenate(ps, axis=0), kv, preferred_element_type=f32)
            acc_scr[g0:g0 + DSA_HG * QB, :] = (acc_scr[g0:g0 + DSA_HG * QB, :]
                                               * jnp.concatenate(alphas, axis=0) + pv)
        return carry

    lax.fori_loop(0, n_kt, att_body, 0)

    for g in range(H // 2):
        parts = []
        for hh in (2 * g, 2 * g + 1):
            inv = 1.0 / l_scr[hh * QB:(hh + 1) * QB, :]
            ol = acc_scr[hh * QB:(hh + 1) * QB, :] * _rep(inv, C_KV_RANK // LANES, axis=1)
            parts.append(jnp.dot(ol.astype(jnp.bfloat16), wuvp_ref[hh], preferred_element_type=f32))
        o_ref[0, :, g * LANES:(g + 1) * LANES] = parts[0] + parts[1]


def _dsa_proj_kernel(x_ref, wq_ref, wkv_ref, wkvT_ref, wqi_ref, wkiT_ref, wwi_ref, qn_ref, kvn_ref, kvnT_ref,
                     cq_ref, ckv_ref, ckvT_ref, qidx_ref, kidxT_ref, widx_ref):
    f32 = jnp.float32
    bf = jnp.bfloat16
    xb = x_ref[...].astype(bf)

    def rms(z, g):
        return z * lax.rsqrt(jnp.mean(z * z, axis=1, keepdims=True) + 1e-6) * g

    cq_ref[...] = rms(jnp.dot(xb, wq_ref[...], preferred_element_type=f32), qn_ref[...]).astype(bf)
    ckv = rms(jnp.dot(xb, wkv_ref[...], preferred_element_type=f32), kvn_ref[...])
    ckv_ref[0] = ckv.astype(bf)
    nt = (((1,), (1,)), ((), ()))
    zT = lax.dot_general(wkvT_ref[...], xb, nt, preferred_element_type=f32)
    ssq = jnp.sum(zT * zT, axis=0, keepdims=True) * (1.0 / C_KV_RANK)
    ckvT_ref[0] = (zT * lax.rsqrt(ssq + 1e-6) * kvnT_ref[...]).astype(bf)
    zqi = jnp.dot(xb, wqi_ref[...], preferred_element_type=f32)
    for h in range(IDX_HEADS):
        qidx_ref[0, h] = zqi[:, h * IDX_DIM:(h + 1) * IDX_DIM].astype(bf)
    kidxT_ref[0] = lax.dot_general(wkiT_ref[...], xb, nt, preferred_element_type=f32).astype(bf)
    widx_ref[...] = jnp.dot(xb, wwi_ref[...], preferred_element_type=f32)[:, 0:IDX_HEADS]


def _dsa_proj(x, w_in, q_norm, kv_norm):
    B, T, D = x.shape
    N = B * T
    TK = DSA_TK
    NT = T // TK
    bf = jnp.bfloat16
    s1 = C_Q_RANK
    s2 = s1 + C_KV_RANK
    s3 = s2 + IDX_HEADS * IDX_DIM
    s4 = s3 + IDX_DIM
    wq = w_in[:, :s1].astype(bf)
    wkv = w_in[:, s1:s2].astype(bf)
    wqi = w_in[:, s2:s3].astype(bf)
    wkvT = wkv.T
    wkiT = w_in[:, s3:s4].T.astype(bf)
    wwi = jnp.pad(w_in[:, s4:], ((0, 0), (0, LANES - IDX_HEADS))).astype(bf)
    cs = lambda shape: pl.BlockSpec(shape, lambda i: (0,) * len(shape))
    cq, ckv, ckvT, qidx, kidxT, widx = pl.pallas_call(
        _dsa_proj_kernel,
        grid=(N // TK,),
        in_specs=[pl.BlockSpec((TK, D), lambda i: (i, 0)),
                  cs(wq.shape), cs(wkv.shape), cs(wkvT.shape), cs(wqi.shape), cs(wkiT.shape), cs(wwi.shape),
                  cs((1, C_Q_RANK)), cs((1, C_KV_RANK)), cs((C_KV_RANK, 1))],
        out_specs=[pl.BlockSpec((TK, C_Q_RANK), lambda i: (i, 0)),
                   pl.BlockSpec((1, TK, C_KV_RANK), lambda i: (i, 0, 0)),
                   pl.BlockSpec((1, C_KV_RANK, TK), lambda i: (i, 0, 0)),
                   pl.BlockSpec((1, IDX_HEADS, TK, IDX_DIM), lambda i: (i // NT, 0, i % NT, 0)),
                   pl.BlockSpec((1, IDX_DIM, TK), lambda i: (i, 0, 0)),
                   pl.BlockSpec((TK, IDX_HEADS), lambda i: (i, 0))],
        out_shape=[jax.ShapeDtypeStruct((N, C_Q_RANK), bf),
                   jax.ShapeDtypeStruct((B * NT, TK, C_KV_RANK), bf),
                   jax.ShapeDtypeStruct((B * NT, C_KV_RANK, TK), bf),
                   jax.ShapeDtypeStruct((B, IDX_HEADS, T, IDX_DIM), bf),
                   jax.ShapeDtypeStruct((B * NT, IDX_DIM, TK), bf),
                   jax.ShapeDtypeStruct((N, IDX_HEADS), jnp.float32)],
        compiler_params=pltpu.CompilerParams(dimension_semantics=("arbitrary",),
                                             vmem_limit_bytes=VMEM_LIMIT),
        name="dsa_proj",
    )(x.reshape(N, D), wq, wkv, wkvT, wqi, wkiT, wwi, q_norm[None, :], kv_norm[None, :], kv_norm[:, None])
    return (cq.reshape(B, T, C_Q_RANK), qidx, widx.reshape(B, T, IDX_HEADS),
            kidxT.reshape(B, NT, IDX_DIM, TK), ckvT.reshape(B, NT, C_KV_RANK, TK),
            ckv.reshape(B, NT, TK, C_KV_RANK))


def _dsa_attention(cq, qidx, w_idx, kidxT, ckvT, ckv, w_uq, w_uk, w_uv):
    B, T, _ = cq.shape
    topk = min(IDX_TOPK_MAX, T // 4)
    TK = DSA_TK
    NT = T // TK
    assert T % (2 * TK) == 0, "counting passes walk key tiles in pairs"
    H = C_HEADS
    bf = jnp.bfloat16
    slopes2 = jnp.exp2(-8.0 * jnp.arange(1, H + 1, dtype=jnp.float32) / H) * LOG2E
    wuq = w_uq.reshape(C_Q_RANK, H, C_HEAD_DIM).transpose(1, 0, 2).astype(bf)
    wukT = w_uk.transpose(1, 2, 0).astype(bf)
    wuv = w_uv.transpose(1, 0, 2)
    zeros = jnp.zeros_like(wuv)
    even = (jnp.arange(H) % 2 == 0)[:, None, None]
    wuvp = jnp.where(even, jnp.concatenate([wuv, zeros], -1), jnp.concatenate([zeros, wuv], -1)).astype(bf)

    kern = functools.partial(_dsa_kernel, topk=topk, seq_len=T)
    full = lambda shape: pl.BlockSpec(shape, lambda b, q, *_: (0,) * len(shape))
    grid_spec = pltpu.PrefetchScalarGridSpec(
        num_scalar_prefetch=0,
        grid=(B, T // QB),
        in_specs=[
            pl.BlockSpec(memory_space=pltpu.SMEM),
            pl.BlockSpec((1, QB, C_Q_RANK), lambda b, q: (b, q, 0)),
            pl.BlockSpec((1, IDX_HEADS, QB, IDX_DIM), lambda b, q: (b, 0, q, 0)),
            pl.BlockSpec((1, QB, IDX_HEADS), lambda b, q: (b, q, 0)),
            pl.BlockSpec((1, NT, IDX_DIM, TK), lambda b, q: (b, 0, 0, 0)),
            pl.BlockSpec((1, NT, C_KV_RANK, TK), lambda b, q: (b, 0, 0, 0)),
            pl.BlockSpec((1, NT, TK, C_KV_RANK), lambda b, q: (b, 0, 0, 0)),
            full((H, C_Q_RANK, C_HEAD_DIM)),
            full((H, C_HEAD_DIM, C_KV_RANK)),
            full((H, C_KV_RANK, LANES)),
        ],
        out_specs=pl.BlockSpec((1, QB, H * C_HEAD_DIM), lambda b, q: (b, q, 0)),
        scratch_shapes=[
            pltpu.VMEM((NT, QB, TK), jnp.int32),
            pltpu.VMEM((IDX_HEADS, QB, LANES), jnp.float32),
            pltpu.VMEM((H * QB, C_KV_RANK), bf),
            pltpu.VMEM((QB, TK), jnp.float32),
            pltpu.VMEM((H * QB, LANES), jnp.float32),
            pltpu.VMEM((H * QB, LANES), jnp.float32),
            pltpu.VMEM((H * QB, C_KV_RANK), jnp.float32),
            pltpu.VMEM((QB, LANES), jnp.int32),
            pltpu.VMEM((QB, LANES), jnp.int32),
        ],
    )
    return pl.pallas_call(
        kern,
        grid_spec=grid_spec,
        out_shape=jax.ShapeDtypeStruct((B, T, H * C_HEAD_DIM), jnp.float32),
        compiler_params=pltpu.CompilerParams(dimension_semantics=("arbitrary", "arbitrary"),
                                             vmem_limit_bytes=56 * 1024 * 1024),
        name="dsa_attention",
    )(slopes2, cq, qidx, w_idx, kidxT, ckvT, ckv, wuq, wukT, wuvp)


def _mixer_c(x, w_in, q_norm, kv_norm, w_uq, w_uk, w_uv):
    return _dsa_attention(*_dsa_proj(x, w_in, q_norm, kv_norm), w_uq, w_uk, w_uv)


def _moe_layer(layer, x1, route, cnt, routeT, p, w_gu, w_down, ln_g, ln_b, w_gate, w_proj, xs_init):
    N = x1.shape[0]
    M = N * TOPK_IN_GROUP
    flat_e = routeT[0:TOPK_IN_GROUP].astype(jnp.int32).T.reshape(M)
    rank = routeT[4:4 + TOPK_IN_GROUP].astype(jnp.int32).T.reshape(M)
    counts = cnt[0, N_GROUPS:N_GROUPS + N_EXPERTS].astype(jnp.int32)
    return _moe_experts(layer, x1, flat_e, rank, counts, route, p, w_gu, w_down, ln_g, ln_b, w_gate, w_proj,
                        xs_init)


MOE_BLK = 512
MOE_TM = 512
MOE_DMA_UNROLL = 8


def _dispatch_kernel(dest_ref, x_ref, xs_init_ref, xs_ref, sem):
    del xs_init_ref
    base = pl.program_id(0) * (TOPK_IN_GROUP * MOE_TM)

    def row_copy(t, k):
        d = dest_ref[base + TOPK_IN_GROUP * t + k]
        return pltpu.make_async_copy(x_ref.at[pl.ds(t, 1), :], xs_ref.at[pl.ds(d, 1), :], sem)

    def issue(t, c):
        for k in range(TOPK_IN_GROUP):
            row_copy(t, k).start(priority=k)
        return c

    def drain(t, c):
        for k in range(TOPK_IN_GROUP):
            row_copy(t, k).wait()
        return c

    lax.fori_loop(0, MOE_TM, issue, 0, unroll=MOE_DMA_UNROLL)
    lax.fori_loop(0, MOE_TM, drain, 0, unroll=MOE_DMA_UNROLL)


def _expert_kernel(be_ref, nu_ref, xs_ref, wgu_ref, wd_ref, ys_ref, wgu_bf, wd_bf):
    b = pl.program_id(0)
    F = wd_bf.shape[0]

    @pl.when(b < nu_ref[0])
    def _():
        e = be_ref[b]
        prev = be_ref[jnp.maximum(b - 1, 0)]

        @pl.when((b == 0) | (e != prev))
        def _():
            wgu_bf[...] = wgu_ref[0, 0].astype(jnp.bfloat16)
            wd_bf[...] = wd_ref[0, 0].astype(jnp.bfloat16)

        h = jnp.dot(xs_ref[...].astype(jnp.bfloat16), wgu_bf[...], preferred_element_type=jnp.float32)
        a = h[:, :F]
        act = a * jax.nn.sigmoid(a) * h[:, F:]
        ys_ref[...] = jnp.dot(act.astype(jnp.bfloat16), wd_bf[...], preferred_element_type=jnp.float32)

    @pl.when(b >= nu_ref[0])
    def _():
        ys_ref[...] = jnp.zeros(ys_ref.shape, ys_ref.dtype)


def _combine_kernel(dest_ref, ys_ref, route_ref, x1_ref, p_ref, lng_ref, lnb_ref, wg_ref, wp_ref,
                    o_ref, buf, sem):
    i = pl.program_id(0)
    n = pl.num_programs(0)

    def row_copy(step, slot, t, k):
        d = dest_ref[step * (TOPK_IN_GROUP * MOE_TM) + TOPK_IN_GROUP * t + k]
        return pltpu.make_async_copy(ys_ref.at[pl.ds(d, 1), :], buf.at[slot, k, pl.ds(t, 1), :], sem.at[slot])

    def issue(step, slot):
        def body(t, c):
            for k in range(TOPK_IN_GROUP):
                row_copy(step, slot, t, k).start(priority=k)
            return c
        lax.fori_loop(0, MOE_TM, body, 0, unroll=MOE_DMA_UNROLL)

    def drain(step, slot):
        def body(t, c):
            for k in range(TOPK_IN_GROUP):
                row_copy(step, slot, t, k).wait()
            return c
        lax.fori_loop(0, MOE_TM, body, 0, unroll=MOE_DMA_UNROLL)

    @pl.when(i == 0)
    def _():
        issue(0, 0)

    @pl.when(i + 1 < n)
    def _():
        issue(i + 1, (i + 1) % 2)

    slot = i % 2
    drain(i, slot)
    g = route_ref[...]
    m = g[:, 2:3] * buf[slot, 0] + g[:, 3:4] * buf[slot, 1]
    x2 = _ln_rows(ALPHA * x1_ref[...] + m, lng_ref[...], lnb_ref[...])
    o_ref[...] = x2 + jax.nn.sigmoid(_bdot(x2, wg_ref[...])) * _bdot(p_ref[...], wp_ref[...])


def _moe_experts(layer, xt, flat_e, rank, counts, route, p, w_gu, w_down, ln_g, ln_b, w_gate, w_proj, xs_init):
    N, D = xt.shape
    M = flat_e.shape[0]
    F = w_down.shape[2]
    BLK, TM = MOE_BLK, MOE_TM
    nb = -(-(M + N_EXPERTS * (BLK - 1)) // BLK)
    R = nb * BLK
    padded = ((counts + BLK - 1) // BLK) * BLK
    pend = jnp.cumsum(padded)
    pstart = pend - padded
    dest = (pstart[flat_e] + rank).astype(jnp.int32)
    nused = (pend[-1:] // BLK).astype(jnp.int32)
    blk_e = jnp.minimum(jnp.sum(pend[None, :] <= (jnp.arange(nb) * BLK)[:, None], axis=1),
                        N_EXPERTS - 1).astype(jnp.int32)
    vmem = 56 * 1024 * 1024

    xs = pl.pallas_call(
        _dispatch_kernel,
        grid_spec=pltpu.PrefetchScalarGridSpec(
            num_scalar_prefetch=1,
            grid=(N // TM,),
            in_specs=[pl.BlockSpec((TM, D), lambda i, d: (i, 0)),
                      pl.BlockSpec(memory_space=pl.ANY)],
            out_specs=pl.BlockSpec(memory_space=pl.ANY),
            scratch_shapes=[pltpu.SemaphoreType.DMA(())],
        ),
        out_shape=jax.ShapeDtypeStruct((R, D), jnp.float32),
        input_output_aliases={2: 0},
        compiler_params=pltpu.CompilerParams(dimension_semantics=("arbitrary",), vmem_limit_bytes=vmem),
        name="moe_dispatch",
    )(dest, xt, jnp.zeros((R, D), jnp.float32) if xs_init is None else xs_init)

    def blk(b, be, nu):
        return jnp.minimum(b, nu[0] - 1)

    ys = pl.pallas_call(
        _expert_kernel,
        grid_spec=pltpu.PrefetchScalarGridSpec(
            num_scalar_prefetch=2,
            grid=(nb,),
            in_specs=[pl.BlockSpec((BLK, D), lambda b, be, nu: (blk(b, be, nu), 0)),
                      pl.BlockSpec((1, 1, D, 2 * F), lambda b, be, nu: (layer, be[blk(b, be, nu)], 0, 0)),
                      pl.BlockSpec((1, 1, F, D), lambda b, be, nu: (layer, be[blk(b, be, nu)], 0, 0))],
            out_specs=pl.BlockSpec((BLK, D), lambda b, be, nu: (b, 0)),
            scratch_shapes=[pltpu.VMEM((D, 2 * F), jnp.bfloat16), pltpu.VMEM((F, D), jnp.bfloat16)],
        ),
        out_shape=jax.ShapeDtypeStruct((R, D), jnp.float32),
        compiler_params=pltpu.CompilerParams(dimension_semantics=("arbitrary",), vmem_limit_bytes=vmem),
        name="moe_experts",
    )(blk_e, nused, xs, w_gu, w_down)

    return xs, pl.pallas_call(
        _combine_kernel,
        grid_spec=pltpu.PrefetchScalarGridSpec(
            num_scalar_prefetch=1,
            grid=(N // TM,),
            in_specs=[pl.BlockSpec(memory_space=pl.ANY),
                      pl.BlockSpec((TM, ROUTE_W), lambda i, d: (i, 0)),
                      pl.BlockSpec((TM, D), lambda i, d: (i, 0)),
                      pl.BlockSpec((TM, p.shape[1]), lambda i, d: (i, 0)),
                      pl.BlockSpec((1, D), lambda i, d: (0, 0)),
                      pl.BlockSpec((1, D), lambda i, d: (0, 0)),
                      pl.BlockSpec((D, D), lambda i, d: (0, 0)),
                      pl.BlockSpec((p.shape[1], D), lambda i, d: (0, 0))],
            out_specs=pl.BlockSpec((TM, D), lambda i, d: (i, 0)),
            scratch_shapes=[pltpu.VMEM((2, TOPK_IN_GROUP, TM, D), jnp.float32),
                            pltpu.SemaphoreType.DMA((2,))],
        ),
        out_shape=jax.ShapeDtypeStruct((N, D), jnp.float32),
        compiler_params=pltpu.CompilerParams(dimension_semantics=("arbitrary",), vmem_limit_bytes=vmem),
        name="moe_combine",
    )(dest, ys, route, xt, p, ln_g[None, :], ln_b[None, :],
      w_gate.astype(jnp.bfloat16), w_proj.astype(jnp.bfloat16))


def kernel(x, p, ab_w_in, ret_gn_g, pool_w, pool_scale, ab_w_out, c_w_in, c_q_norm, c_kv_norm, c_w_uq, c_w_uk, c_w_uv, c_w_out, ln1_g, ln1_b, ln2_g, ln2_b, moe_w_group, moe_b_group, moe_w_expert, moe_b_expert, moe_w_gu, moe_w_down, ple_w_gate, ple_w_proj):
    B, T, D = x.shape
    N = B * T
    xs_buf = None
    for i in range(DEPTH):
        j = i // 2
        wr, br = _router_weights(moe_w_group[i], moe_b_group[i], moe_w_expert[i], moe_b_expert[i])
        if i % 2 == 0:
            x1, route, cnt, routeT = _mixer_ab_layer(x, ab_w_in[j], ret_gn_g[j], pool_w[j], pool_scale[j],
                                                     ab_w_out[j], ln1_g[i], ln1_b[i], wr, br)
            x1, route = x1.reshape(N, D), route.reshape(N, ROUTE_W)
        else:
            o = _mixer_c(x, c_w_in[j], c_q_norm[j], c_kv_norm[j], c_w_uq[j], c_w_uk[j], c_w_uv[j])
            x1, route, cnt, routeT = _outproj_layer(o.reshape(N, -1), x.reshape(N, D), c_w_out[j],
                                                    ln1_g[i], ln1_b[i], wr, br)
        xs_buf, x = _moe_layer(i, x1, route, cnt, routeT, p[i].reshape(N, -1), moe_w_gu, moe_w_down,
                               ln2_g[i], ln2_b[i], ple_w_gate[i], ple_w_proj[i], xs_buf)
        x = x.reshape(B, T, D)
    return x
```

```python
import functools
import jax
import jax.numpy as jnp
from jax import lax
from jax.experimental import pallas as pl
from jax.experimental.pallas import tpu as pltpu

DEPTH = 2
RET_HEADS = 4
RET_DK = 128
RET_DV = 128
RET_CHUNK = 128
POOL_WINDOWS = (2, 4, 8, 16)
POOL_GROUP = 128
C_HEADS = 16
C_HEAD_DIM = 64
C_Q_RANK = 256
C_KV_RANK = 256
IDX_HEADS = 8
IDX_DIM = 64
IDX_TOPK_MAX = 256
Q_BLOCK = 128
N_GROUPS = 4
EXPERTS_PER_GROUP = 8
N_EXPERTS = N_GROUPS * EXPERTS_PER_GROUP
TOPK_IN_GROUP = 2
ALPHA = (2.0 * DEPTH) ** 0.25
LN_EPS = 1e-5

TOK_TM = 512
POOL_TAIL = 16
VMEM_LIMIT = 56 * 1024 * 1024
ROUTE_W = 128
ROUTE_ROWS = 8


def _bdot(a, b):
    return jnp.dot(a.astype(jnp.bfloat16), b.astype(jnp.bfloat16), preferred_element_type=jnp.float32)


def _ln_rows(v, g, b):
    mu = jnp.mean(v, axis=1, keepdims=True)
    d = v - mu
    var = jnp.mean(d * d, axis=1, keepdims=True)
    return d * lax.rsqrt(var + LN_EPS) * g + b


def _route_rows(x1, wr_ref, br_ref, tri_ref, run_scr, cnt_ref):
    f32 = jnp.float32
    logits = _bdot(x1, wr_ref[...]) + br_ref[...]
    lane = lax.broadcasted_iota(jnp.int32, logits.shape, 1)
    lane_f = lane.astype(f32)
    ninf = -jnp.inf
    gl = jnp.where(lane < N_GROUPS, logits, ninf)
    gmax = jnp.max(gl, axis=1, keepdims=True)
    grp = jnp.min(jnp.where(gl == gmax, lane_f, float(ROUTE_W)), axis=1, keepdims=True)
    g_w = 1.0 / jnp.sum(jnp.exp(gl - gmax), axis=1, keepdims=True)
    egrp = ((lane - N_GROUPS) >> 3).astype(f32)
    el = jnp.where(lane >= N_GROUPS, jnp.where(egrp == grp, logits, ninf), ninf)
    v1 = jnp.max(el, axis=1, keepdims=True)
    i1 = jnp.min(jnp.where(el == v1, lane_f, float(ROUTE_W)), axis=1, keepdims=True)
    el2 = jnp.where(lane_f == i1, ninf, el)
    v2 = jnp.max(el2, axis=1, keepdims=True)
    i2 = jnp.min(jnp.where(el2 == v2, lane_f, float(ROUTE_W)), axis=1, keepdims=True)
    e = jnp.exp(v2 - v1)
    den = 1.0 / (1.0 + e)
    oh1 = jnp.where(lane_f == i1, 1.0, 0.0)
    oh2 = jnp.where(lane_f == i2, 1.0, 0.0)
    oh = oh1 + oh2
    base = run_scr[0:1, :] + jnp.dot(tri_ref[...], oh.astype(jnp.bfloat16), preferred_element_type=f32)
    r1 = jnp.sum(oh1 * base, axis=1, keepdims=True)
    r2 = jnp.sum(oh2 * base, axis=1, keepdims=True)
    run_new = run_scr[0:1, :] + jnp.sum(oh, axis=0, keepdims=True)
    run_scr[0:1, :] = run_new
    cnt_ref[...] = jnp.broadcast_to(run_new, cnt_ref.shape)
    sel = lambda l, v, rest: jnp.where(lane == l, v, rest)
    return sel(0, i1 - N_GROUPS, sel(1, i2 - N_GROUPS, sel(2, g_w * den, sel(3, g_w * e * den,
                                                                             sel(4, r1, sel(5, r2, 0.0))))))


def _mixer_ab_kernel(cdec_ref, x_ref, win_ref, idec_ref, qdec_ref, kdec_ref, gng_ref, bandc_ref, bandp_ref,
                     pw_ref, psc_ref, wout_ref, lng_ref, lnb_ref, wr_ref, br_ref, tri_ref,
                     x1_ref, route_ref, cnt_ref, routeT_ref, r_scr, tail_scr, mixed_scr, run_scr):
    f32 = jnp.float32
    bf = jnp.bfloat16
    C = RET_CHUNK
    TM = x_ref.shape[1]
    nq = RET_HEADS * RET_DK
    nv = RET_HEADS * RET_DV
    ti = pl.program_id(1)

    @pl.when(ti == 0)
    def _():
        r_scr[...] = jnp.zeros(r_scr.shape, f32)
        tail_scr[...] = jnp.zeros(tail_scr.shape, f32)

    x = x_ref[0]
    xb = x.astype(bf)
    zq = jnp.dot(xb, win_ref[:, 0:nq], preferred_element_type=f32)
    zk = jnp.dot(xb, win_ref[:, nq:2 * nq], preferred_element_type=f32) * (RET_DK ** -0.5)
    zv = jnp.dot(xb, win_ref[:, 2 * nq:2 * nq + nv], preferred_element_type=f32)
    zg = jnp.dot(xb, win_ref[:, 2 * nq + nv:2 * nq + 2 * nv], preferred_element_type=f32)
    zu = jnp.dot(xb, win_ref[:, 2 * nq + 2 * nv:], preferred_element_type=f32)

    for h in range(RET_HEADS):
        hs = slice(h * RET_DK, (h + 1) * RET_DK)
        R = r_scr[h]
        for c in range(TM // C):
            cs = slice(c * C, (c + 1) * C)
            qc, kc, vc = zq[cs, hs], zk[cs, hs], zv[cs, hs]
            s = lax.dot_general(qc.astype(bf), kc.astype(bf), (((1,), (1,)), ((), ())),
                                preferred_element_type=f32) * idec_ref[h]
            o = _bdot(s, vc) + _bdot(qc * qdec_ref[h], R)
            R = R * cdec_ref[h] + lax.dot_general((kc * kdec_ref[h]).astype(bf), vc.astype(bf),
                                                  (((0,), (0,)), ((), ())), preferred_element_type=f32)
            mu = jnp.mean(o, axis=1, keepdims=True)
            d = o - mu
            var = jnp.mean(d * d, axis=1, keepdims=True)
            gate = zg[cs, hs]
            ret = d * lax.rsqrt(var + LN_EPS) * gng_ref[:, hs] * (gate * jax.nn.sigmoid(gate))
            mixed_scr[cs, hs] = ret.astype(bf)
        r_scr[h] = R

    t_glob = ti * TM + lax.broadcasted_iota(jnp.int32, (TM, POOL_GROUP), 0)
    tail = tail_scr[...]
    for gi, w in enumerate(POOL_WINDOWS):
        gs = slice(gi * POOL_GROUP, (gi + 1) * POOL_GROUP)
        u = zu[:, gs]
        u_hi = u.astype(bf)
        u_lo = (u - u_hi.astype(f32)).astype(bf)
        tl = tail[:, gs]
        t_hi = tl.astype(bf)
        t_lo = (tl - t_hi.astype(f32)).astype(bf)
        wsum = (jnp.dot(bandc_ref[gi], u_hi, preferred_element_type=f32)
                + jnp.dot(bandc_ref[gi], u_lo, preferred_element_type=f32)
                + jnp.dot(bandp_ref[gi], t_hi, preferred_element_type=f32)
                + jnp.dot(bandp_ref[gi], t_lo, preferred_element_type=f32))
        cnt = jnp.minimum(t_glob + 1, w).astype(f32)
        dpool = wsum / cnt - u
        y = _bdot(dpool, pw_ref[gi]) * psc_ref[:, gs]
        mixed_scr[:, nv + gi * POOL_GROUP:nv + (gi + 1) * POOL_GROUP] = y.astype(bf)
    tail_scr[...] = zu[TM - POOL_TAIL:, :]

    hmix = jnp.dot(mixed_scr[...], wout_ref[...], preferred_element_type=f32)
    x1 = _ln_rows(ALPHA * x + hmix, lng_ref[...], lnb_ref[...])
    x1_ref[0] = x1
    @pl.when((pl.program_id(0) == 0) & (ti == 0))
    def _():
        run_scr[...] = jnp.zeros(run_scr.shape, f32)

    route = _route_rows(x1, wr_ref, br_ref, tri_ref, run_scr, cnt_ref)
    route_ref[0] = route
    routeT_ref[...] = route.T[0:ROUTE_ROWS, :]


def _const_spec(shape):
    return pl.BlockSpec(shape, lambda b, t: (0,) * len(shape))


def _router_weights(w_group, b_group, w_expert, b_expert):
    D = w_group.shape[0]
    pad = ROUTE_W - N_GROUPS - N_EXPERTS
    wr = jnp.concatenate([w_group, w_expert, jnp.zeros((D, pad), jnp.float32)], axis=1).astype(jnp.bfloat16)
    br = jnp.concatenate([b_group, b_expert, jnp.zeros((pad,), jnp.float32)])[None, :]
    return wr, br


def _mixer_ab_layer(x, w_in, gn_g, pool_w, pool_scale, w_out, ln_g, ln_b, wr, br):
    B, T, D = x.shape
    TM = TOK_TM
    C = RET_CHUNK
    H = RET_HEADS
    bf = jnp.bfloat16
    log_g = jnp.log1p(-jnp.exp2(-5.0 - jnp.arange(H, dtype=jnp.float32)))
    i = jnp.arange(C, dtype=jnp.float32)
    diff = i[:, None] - i[None, :]
    idec = jnp.where(diff >= 0, jnp.exp(log_g[:, None, None] * jnp.maximum(diff, 0.0)), 0.0)
    qdec = jnp.broadcast_to(jnp.exp(log_g[:, None] * (i + 1.0))[:, :, None], (H, C, RET_DK))
    kdec = jnp.broadcast_to(jnp.exp(log_g[:, None] * (C - 1.0 - i))[:, :, None], (H, C, RET_DK))
    cdec = jnp.exp(log_g * C)
    r = jnp.arange(TM)[:, None]
    bandc = jnp.stack([((r - jnp.arange(TM)[None, :] >= 0) & (r - jnp.arange(TM)[None, :] < w))
                       for w in POOL_WINDOWS]).astype(bf)
    jp = jnp.arange(POOL_TAIL)[None, :] - POOL_TAIL
    bandp = jnp.stack([((r - jp) < w) for w in POOL_WINDOWS]).astype(bf)
    nG = len(POOL_WINDOWS)
    return pl.pallas_call(
        _mixer_ab_kernel,
        grid=(B, T // TM),
        in_specs=[pl.BlockSpec(memory_space=pltpu.SMEM),
                  pl.BlockSpec((1, TM, D), lambda b, t: (b, t, 0)),
                  _const_spec(w_in.shape), _const_spec((H, C, C)), _const_spec((H, C, RET_DK)),
                  _const_spec((H, C, RET_DK)), _const_spec((1, H * RET_DV)),
                  _const_spec((nG, TM, TM)), _const_spec((nG, TM, POOL_TAIL)),
                  _const_spec((nG, POOL_GROUP, POOL_GROUP)), _const_spec((1, nG * POOL_GROUP)),
                  _const_spec(w_out.shape), _const_spec((1, D)), _const_spec((1, D)),
                  _const_spec((D, ROUTE_W)), _const_spec((1, ROUTE_W)), _const_spec((TM, TM))],
        out_specs=[pl.BlockSpec((1, TM, D), lambda b, t: (b, t, 0)),
                   pl.BlockSpec((1, TM, ROUTE_W), lambda b, t: (b, t, 0)),
                   _const_spec((8, ROUTE_W)),
                   pl.BlockSpec((ROUTE_ROWS, TM), lambda b, t: (0, b * (T // TM) + t))],
        out_shape=[jax.ShapeDtypeStruct((B, T, D), jnp.float32),
                   jax.ShapeDtypeStruct((B, T, ROUTE_W), jnp.float32),
                   jax.ShapeDtypeStruct((8, ROUTE_W), jnp.float32),
                   jax.ShapeDtypeStruct((ROUTE_ROWS, B * T), jnp.float32)],
        scratch_shapes=[pltpu.VMEM((H, RET_DK, RET_DV), jnp.float32),
                        pltpu.VMEM((POOL_TAIL, nG * POOL_GROUP), jnp.float32),
                        pltpu.VMEM((TM, w_out.shape[0]), bf),
                        pltpu.VMEM((8, ROUTE_W), jnp.float32)],
        compiler_params=pltpu.CompilerParams(dimension_semantics=("arbitrary", "arbitrary"),
                                             vmem_limit_bytes=VMEM_LIMIT),
        name="mixer_ab_layer",
    )(cdec, x, w_in.astype(bf), idec, qdec, kdec, gn_g[None, :], bandc, bandp,
      pool_w.astype(bf), pool_scale[None, :], w_out.astype(bf), ln_g[None, :], ln_b[None, :], wr, br,
      _strict_lower_tri(TM))


def _outproj_kernel(o_ref, x_ref, wout_ref, lng_ref, lnb_ref, wr_ref, br_ref, tri_ref,
                    x1_ref, route_ref, cnt_ref, routeT_ref, run_scr):
    @pl.when(pl.program_id(0) == 0)
    def _():
        run_scr[...] = jnp.zeros(run_scr.shape, jnp.float32)

    h = _bdot(o_ref[...], wout_ref[...])
    x1 = _ln_rows(ALPHA * x_ref[...] + h, lng_ref[...], lnb_ref[...])
    x1_ref[...] = x1
    route = _route_rows(x1, wr_ref, br_ref, tri_ref, run_scr, cnt_ref)
    route_ref[...] = route
    routeT_ref[...] = route.T[0:ROUTE_ROWS, :]


def _strict_lower_tri(n):
    r = jnp.arange(n)
    return (r[:, None] > r[None, :]).astype(jnp.bfloat16)


def _outproj_layer(o, x, w_out, ln_g, ln_b, wr, br):
    N, D = x.shape
    TM = TOK_TM
    cs = lambda shape: pl.BlockSpec(shape, lambda i: (0,) * len(shape))
    return pl.pallas_call(
        _outproj_kernel,
        grid=(N // TM,),
        in_specs=[pl.BlockSpec((TM, o.shape[1]), lambda i: (i, 0)),
                  pl.BlockSpec((TM, D), lambda i: (i, 0)),
                  cs(w_out.shape), cs((1, D)), cs((1, D)), cs((D, ROUTE_W)), cs((1, ROUTE_W)),
                  cs((TM, TM))],
        out_specs=[pl.BlockSpec((TM, D), lambda i: (i, 0)),
                   pl.BlockSpec((TM, ROUTE_W), lambda i: (i, 0)),
                   cs((8, ROUTE_W)),
                   pl.BlockSpec((ROUTE_ROWS, TM), lambda i: (0, i))],
        out_shape=[jax.ShapeDtypeStruct((N, D), jnp.float32),
                   jax.ShapeDtypeStruct((N, ROUTE_W), jnp.float32),
                   jax.ShapeDtypeStruct((8, ROUTE_W), jnp.float32),
                   jax.ShapeDtypeStruct((ROUTE_ROWS, N), jnp.float32)],
        scratch_shapes=[pltpu.VMEM((8, ROUTE_W), jnp.float32)],
        compiler_params=pltpu.CompilerParams(dimension_semantics=("arbitrary",),
                                             vmem_limit_bytes=VMEM_LIMIT),
        name="outproj_layer",
    )(o, x, w_out.astype(jnp.bfloat16), ln_g[None, :], ln_b[None, :], wr, br, _strict_lower_tri(TM))


LANES = 128
QB = Q_BLOCK
DSA_TK = 512
DSA_HG = 4
DSA_VALUE_STEPS = 24
DSA_ROUNDS_PER_TEST = 4
DSA_COARSE_STEPS = 16
NEG_BIG = -1e30
KEY_LO0 = -2139095040
LOG2E = 1.4426950408889634
BIG_IDX = 1 << 30


def _rep(x, n, axis):
    return jnp.concatenate([x] * n, axis=axis) if n > 1 else x


def _sum_lane_chunks(v, width):
    out = v[:, 0:LANES]
    for c in range(1, width // LANES):
        out = out + v[:, c * LANES:(c + 1) * LANES]
    return out


def _dsa_kernel(slopes_ref, cq_ref, qidx_ref, widx_ref, kidxT_ref, ckvT_ref, ckv_ref,
                wuq_ref, wukT_ref, wuvp_ref, o_ref,
                keys_scr, wb_scr, q_scr, a_scr, m_scr, l_scr, acc_scr,
                th_scr, c_scr, *, topk, seq_len):
    f32 = jnp.float32
    TK = DSA_TK
    NCH = TK // LANES
    H = C_HEADS
    qb = pl.program_id(1)
    t0 = qb * QB
    n_kt = (t0 + QB - 1) // TK + 1
    row = lax.broadcasted_iota(jnp.int32, (QB, TK), 0)
    col = lax.broadcasted_iota(jnp.int32, (QB, TK), 1)
    kf = float(topk)

    for h in range(IDX_HEADS):
        wb_scr[h] = jnp.broadcast_to(widx_ref[0, :, h:h + 1], (QB, LANES))

    def idx_body(j, carry):
        kT = kidxT_ref[0, j]
        s_all = jnp.dot(qidx_ref[0].reshape(IDX_HEADS * QB, IDX_DIM), kT, preferred_element_type=f32)
        acc = jnp.zeros((QB, TK), f32)
        for h in range(IDX_HEADS):
            acc = acc + _rep(wb_scr[h], NCH, axis=1) * jnp.maximum(s_all[h * QB:(h + 1) * QB, :], 0.0)
        causal = (j * TK + col) <= (t0 + row)
        sc = jnp.where(causal, acc, -jnp.inf)
        b = pltpu.bitcast(sc, jnp.int32)
        keys_scr[j] = b ^ ((b >> 31) & 0x7FFFFFFF)
        vmax, vmin = carry
        sc_hi = jnp.where(causal, acc, jnp.inf)
        for c in range(NCH):
            vmax = jnp.maximum(vmax, sc[:, c * LANES:(c + 1) * LANES])
            vmin = jnp.minimum(vmin, sc_hi[:, c * LANES:(c + 1) * LANES])
        return vmax, vmin

    vmax, vmin = lax.fori_loop(0, n_kt, idx_body, (jnp.full((QB, LANES), -jnp.inf, f32),
                                                   jnp.full((QB, LANES), jnp.inf, f32)))

    def to_key(v):
        b = pltpu.bitcast(v, jnp.int32)
        return b ^ ((b >> 31) & 0x7FFFFFFF)

    def from_key(k):
        return pltpu.bitcast(k ^ ((k >> 31) & 0x7FFFFFFF), f32)

    @pl.when(n_kt < keys_scr.shape[0])
    def _():
        keys_scr[n_kt] = jnp.full((QB, TK), KEY_LO0 - 1, jnp.int32)

    def count_where(ones_fn):
        def body(j2, c):
            j = 2 * j2
            return (c + _sum_lane_chunks(ones_fn(j, keys_scr[j]), TK)
                    + _sum_lane_chunks(ones_fn(j + 1, keys_scr[j + 1]), TK))
        c = lax.fori_loop(0, (n_kt + 1) // 2, body, jnp.zeros((QB, LANES), f32))
        return jnp.broadcast_to(jnp.sum(c, axis=1, keepdims=True), (QB, LANES))

    n_valid = (t0 + 1 + lax.broadcasted_iota(jnp.int32, (QB, LANES), 0)).astype(f32)
    small = n_valid <= kf
    kmax = to_key(jnp.broadcast_to(jnp.max(vmax, axis=1, keepdims=True), (QB, LANES)))
    kmin = to_key(jnp.broadcast_to(jnp.min(vmin, axis=1, keepdims=True), (QB, LANES)))
    lo0 = jnp.where(small, KEY_LO0, kmin)
    hi0 = jnp.where(small, KEY_LO0 + 1, kmax + 1)

    def row_done(lo, hi, clo):
        return jnp.where(clo == kf, 1.0, jnp.where(hi == lo + 1, 1.0, 0.0))

    def search_round(mid_fn, lo, hi, clo, chi):
        mid = jnp.minimum(jnp.maximum(mid_fn(lo, hi), lo + 1), hi - 1)
        midb = _rep(mid, NCH, axis=1)
        cnt = count_where(lambda j, k: jnp.where(k >= midb, 1.0, 0.0))
        ge = jnp.where(small, kf, cnt) >= kf
        return jnp.where(ge, mid, lo), jnp.where(ge, hi, mid), jnp.where(ge, cnt, clo), jnp.where(ge, chi, cnt)

    def search(mid_fn, rounds_per_test, max_rounds, st):
        def cond(s):
            return (s[1] > 0) & (s[0] < max_rounds)

        def body(s):
            it, _, lo, hi, clo, chi = s
            for _ in range(rounds_per_test):
                lo, hi, clo, chi = search_round(mid_fn, lo, hi, clo, chi)
            pending = (jnp.min(row_done(lo, hi, clo)) < 0.5).astype(jnp.int32)
            return it + rounds_per_test, pending, lo, hi, clo, chi

        return lax.while_loop(cond, body, st)

    value_mid = lambda lo, hi: to_key(0.5 * from_key(lo) + 0.5 * from_key(hi - 1))
    key_mid = lambda lo, hi: (lo & hi) + ((lo ^ hi) >> 1)
    pending0 = (jnp.min(row_done(lo0, hi0, n_valid)) < 0.5).astype(jnp.int32)
    st = (jnp.int32(0), pending0, lo0, hi0, n_valid, jnp.zeros((QB, LANES), f32))
    st = search(value_mid, DSA_ROUNDS_PER_TEST, DSA_COARSE_STEPS, st)
    st = search(value_mid, DSA_ROUNDS_PER_TEST // 2, DSA_VALUE_STEPS, st)
    st = search(key_mid, DSA_ROUNDS_PER_TEST, 32 + DSA_ROUNDS_PER_TEST, (jnp.int32(0),) + st[1:])
    _, _, theta, _, n_ge, n_gt = st
    th_scr[...] = theta
    thb = _rep(theta, NCH, axis=1)

    need = kf - n_gt
    excess = n_ge > kf
    c_scr[...] = jnp.full((QB, LANES), BIG_IDX, jnp.int32)

    @pl.when(jnp.max(jnp.where(excess, 1.0, 0.0)) > 0.0)
    def _():
        def tie_body(i, carry):
            lo, hi = carry
            mid = (lo + hi) >> 1
            midb = _rep(mid, NCH, axis=1)
            cnt = count_where(lambda j, k: jnp.where(
                k == thb, jnp.where((j * TK + col) <= midb, 1.0, 0.0), 0.0))
            ok = cnt >= need
            return jnp.where(ok, lo, mid + 1), jnp.where(ok, mid, hi)
        nbits = max(1, (seq_len - 1).bit_length())
        lo, _ = lax.fori_loop(0, nbits, tie_body,
                              (jnp.zeros((QB, LANES), jnp.int32),
                               jnp.full((QB, LANES), seq_len - 1, jnp.int32)))
        c_scr[...] = jnp.where(excess, lo, BIG_IDX)

    qh_all = jnp.dot(cq_ref[0], wuq_ref[...], preferred_element_type=f32).astype(jnp.bfloat16)
    for h in range(H):
        qh = qh_all[:, h * C_HEAD_DIM:(h + 1) * C_HEAD_DIM]
        qa = jnp.dot(qh, wukT_ref[h], preferred_element_type=f32) * (C_HEAD_DIM ** -0.5 * LOG2E)
        q_scr[h * QB:(h + 1) * QB, :] = qa.astype(jnp.bfloat16)

    m_scr[...] = jnp.full(m_scr.shape, NEG_BIG, f32)
    l_scr[...] = jnp.zeros(l_scr.shape, f32)
    acc_scr[...] = jnp.zeros(acc_scr.shape, f32)

    def att_body(j, carry):
        kT = ckvT_ref[0, j]
        kv = ckv_ref[0, j]
        k = keys_scr[j]
        thb_ = _rep(th_scr[...], NCH, axis=1)
        cb_ = _rep(c_scr[...], NCH, axis=1)
        spos = j * TK + col
        keep_tie = jnp.where(spos <= cb_, 0.0, NEG_BIG)
        a_scr[...] = jnp.where(k > thb_, 0.0, jnp.where(k == thb_, keep_tie, NEG_BIG))
        srel8 = (j * TK - t0 + lax.broadcasted_iota(jnp.int32, (8, TK), 1)).astype(f32)

        for g in range(H // DSA_HG):
            g0 = g * DSA_HG * QB
            lg = jnp.dot(q_scr[g0:g0 + DSA_HG * QB, :], kT, preferred_element_type=f32)
            ps, alphas = [], []
            for u in range(DSA_HG):
                h = g * DSA_HG + u
                r = slice(h * QB, (h + 1) * QB)
                bh = _rep(slopes_ref[h] * srel8, QB // 8, axis=0)
                l = lg[u * QB:(u + 1) * QB, :] + a_scr[...] + bh
                m_old = m_scr[r, :]
                m_new = jnp.maximum(m_old, jnp.max(l, axis=1, keepdims=True))
                p = jnp.exp2(l - _rep(m_new, NCH, axis=1))
                alpha = jnp.exp2(m_old - m_new)
                rs = jnp.sum(_sum_lane_chunks(p, TK), axis=1, keepdims=True)
                l_scr[r, :] = alpha * l_scr[r, :] + rs
                m_scr[r, :] = m_new
                ps.append(p.astype(jnp.bfloat16))
                alphas.append(_rep(alpha, C_KV_RANK // LANES, axis=1))
            pv = jnp.dot(jnp.concatenate(ps, axis=0), kv, preferred_element_type=f32)
            acc_scr[g0:g0 + DSA_HG * QB, :] = (acc_scr[g0:g0 + DSA_HG * QB, :]
                                               * jnp.concatenate(alphas, axis=0) + pv)
        return carry

    lax.fori_loop(0, n_kt, att_body, 0)

    for g in range(H // 2):
        parts = []
        for hh in (2 * g, 2 * g + 1):
            inv = 1.0 / l_scr[hh * QB:(hh + 1) * QB, :]
            ol = acc_scr[hh * QB:(hh + 1) * QB, :] * _rep(inv, C_KV_RANK // LANES, axis=1)
            parts.append(jnp.dot(ol.astype(jnp.bfloat16), wuvp_ref[hh], preferred_element_type=f32))
        o_ref[0, :, g * LANES:(g + 1) * LANES] = parts[0] + parts[1]


def _dsa_proj_kernel(x_ref, wq_ref, wkv_ref, wkvT_ref, wqi_ref, wkiT_ref, wwi_ref, qn_ref, kvn_ref, kvnT_ref,
                     cq_ref, ckv_ref, ckvT_ref, qidx_ref, kidxT_ref, widx_ref):
    f32 = jnp.float32
    bf = jnp.bfloat16
    xb = x_ref[...].astype(bf)

    def rms(z, g):
        return z * lax.rsqrt(jnp.mean(z * z, axis=1, keepdims=True) + 1e-6) * g

    cq_ref[...] = rms(jnp.dot(xb, wq_ref[...], preferred_element_type=f32), qn_ref[...]).astype(bf)
    ckv = rms(jnp.dot(xb, wkv_ref[...], preferred_element_type=f32), kvn_ref[...])
    ckv_ref[0] = ckv.astype(bf)
    nt = (((1,), (1,)), ((), ()))
    zT = lax.dot_general(wkvT_ref[...], xb, nt, preferred_element_type=f32)
    ssq = jnp.sum(zT * zT, axis=0, keepdims=True) * (1.0 / C_KV_RANK)
    ckvT_ref[0] = (zT * lax.rsqrt(ssq + 1e-6) * kvnT_ref[...]).astype(bf)
    zqi = jnp.dot(xb, wqi_ref[...], preferred_element_type=f32)
    for h in range(IDX_HEADS):
        qidx_ref[0, h] = zqi[:, h * IDX_DIM:(h + 1) * IDX_DIM].astype(bf)
    kidxT_ref[0] = lax.dot_general(wkiT_ref[...], xb, nt, preferred_element_type=f32).astype(bf)
    widx_ref[...] = jnp.dot(xb, wwi_ref[...], preferred_element_type=f32)[:, 0:IDX_HEADS]


def _dsa_proj(x, w_in, q_norm, kv_norm):
    B, T, D = x.shape
    N = B * T
    TK = DSA_TK
    NT = T // TK
    bf = jnp.bfloat16
    s1 = C_Q_RANK
    s2 = s1 + C_KV_RANK
    s3 = s2 + IDX_HEADS * IDX_DIM
    s4 = s3 + IDX_DIM
    wq = w_in[:, :s1].astype(bf)
    wkv = w_in[:, s1:s2].astype(bf)
    wqi = w_in[:, s2:s3].astype(bf)
    wkvT = wkv.T
    wkiT = w_in[:, s3:s4].T.astype(bf)
    wwi = jnp.pad(w_in[:, s4:], ((0, 0), (0, LANES - IDX_HEADS))).astype(bf)
    cs = lambda shape: pl.BlockSpec(shape, lambda i: (0,) * len(shape))
    cq, ckv, ckvT, qidx, kidxT, widx = pl.pallas_call(
        _dsa_proj_kernel,
        grid=(N // TK,),
        in_specs=[pl.BlockSpec((TK, D), lambda i: (i, 0)),
                  cs(wq.shape), cs(wkv.shape), cs(wkvT.shape), cs(wqi.shape), cs(wkiT.shape), cs(wwi.shape),
                  cs((1, C_Q_RANK)), cs((1, C_KV_RANK)), cs((C_KV_RANK, 1))],
        out_specs=[pl.BlockSpec((TK, C_Q_RANK), lambda i: (i, 0)),
                   pl.BlockSpec((1, TK, C_KV_RANK), lambda i: (i, 0, 0)),
                   pl.BlockSpec((1, C_KV_RANK, TK), lambda i: (i, 0, 0)),
                   pl.BlockSpec((1, IDX_HEADS, TK, IDX_DIM), lambda i: (i // NT, 0, i % NT, 0)),
                   pl.BlockSpec((1, IDX_DIM, TK), lambda i: (i, 0, 0)),
                   pl.BlockSpec((TK, IDX_HEADS), lambda i: (i, 0))],
        out_shape=[jax.ShapeDtypeStruct((N, C_Q_RANK), bf),
                   jax.ShapeDtypeStruct((B * NT, TK, C_KV_RANK), bf),
                   jax.ShapeDtypeStruct((B * NT, C_KV_RANK, TK), bf),
                   jax.ShapeDtypeStruct((B, IDX_HEADS, T, IDX_DIM), bf),
                   jax.ShapeDtypeStruct((B * NT, IDX_DIM, TK), bf),
                   jax.ShapeDtypeStruct((N, IDX_HEADS), jnp.float32)],
        compiler_params=pltpu.CompilerParams(dimension_semantics=("arbitrary",),
                                             vmem_limit_bytes=VMEM_LIMIT),
        name="dsa_proj",
    )(x.reshape(N, D), wq, wkv, wkvT, wqi, wkiT, wwi, q_norm[None, :], kv_norm[None, :], kv_norm[:, None])
    return (cq.reshape(B, T, C_Q_RANK), qidx, widx.reshape(B, T, IDX_HEADS),
            kidxT.reshape(B, NT, IDX_DIM, TK), ckvT.reshape(B, NT, C_KV_RANK, TK),
            ckv.reshape(B, NT, TK, C_KV_RANK))


def _dsa_attention(cq, qidx, w_idx, kidxT, ckvT, ckv, w_uq, w_uk, w_uv):
    B, T, _ = cq.shape
    topk = min(IDX_TOPK_MAX, T // 4)
    TK = DSA_TK
    NT = T // TK
    assert T % (2 * TK) == 0, "counting passes walk key tiles in pairs"
    H = C_HEADS
    bf = jnp.bfloat16
    slopes2 = jnp.exp2(-8.0 * jnp.arange(1, H + 1, dtype=jnp.float32) / H) * LOG2E
    wuq = w_uq.astype(bf)
    wukT = w_uk.transpose(1, 2, 0).astype(bf)
    wuv = w_uv.transpose(1, 0, 2)
    zeros = jnp.zeros_like(wuv)
    even = (jnp.arange(H) % 2 == 0)[:, None, None]
    wuvp = jnp.where(even, jnp.concatenate([wuv, zeros], -1), jnp.concatenate([zeros, wuv], -1)).astype(bf)

    kern = functools.partial(_dsa_kernel, topk=topk, seq_len=T)
    full = lambda shape: pl.BlockSpec(shape, lambda b, q, *_: (0,) * len(shape))
    grid_spec = pltpu.PrefetchScalarGridSpec(
        num_scalar_prefetch=0,
        grid=(B, T // QB),
        in_specs=[
            pl.BlockSpec(memory_space=pltpu.SMEM),
            pl.BlockSpec((1, QB, C_Q_RANK), lambda b, q: (b, q, 0)),
            pl.BlockSpec((1, IDX_HEADS, QB, IDX_DIM), lambda b, q: (b, 0, q, 0)),
            pl.BlockSpec((1, QB, IDX_HEADS), lambda b, q: (b, q, 0)),
            pl.BlockSpec((1, NT, IDX_DIM, TK), lambda b, q: (b, 0, 0, 0)),
            pl.BlockSpec((1, NT, C_KV_RANK, TK), lambda b, q: (b, 0, 0, 0)),
            pl.BlockSpec((1, NT, TK, C_KV_RANK), lambda b, q: (b, 0, 0, 0)),
            full((C_Q_RANK, H * C_HEAD_DIM)),
            full((H, C_HEAD_DIM, C_KV_RANK)),
            full((H, C_KV_RANK, LANES)),
        ],
        out_specs=pl.BlockSpec((1, QB, H * C_HEAD_DIM), lambda b, q: (b, q, 0)),
        scratch_shapes=[
            pltpu.VMEM((NT, QB, TK), jnp.int32),
            pltpu.VMEM((IDX_HEADS, QB, LANES), jnp.float32),
            pltpu.VMEM((H * QB, C_KV_RANK), bf),
            pltpu.VMEM((QB, TK), jnp.float32),
            pltpu.VMEM((H * QB, LANES), jnp.float32),
            pltpu.VMEM((H * QB, LANES), jnp.float32),
            pltpu.VMEM((H * QB, C_KV_RANK), jnp.float32),
            pltpu.VMEM((QB, LANES), jnp.int32),
            pltpu.VMEM((QB, LANES), jnp.int32),
        ],
    )
    return pl.pallas_call(
        kern,
        grid_spec=grid_spec,
        out_shape=jax.ShapeDtypeStruct((B, T, H * C_HEAD_DIM), jnp.float32),
        compiler_params=pltpu.CompilerParams(dimension_semantics=("arbitrary", "arbitrary"),
                                             vmem_limit_bytes=VMEM_LIMIT),
        name="dsa_attention",
    )(slopes2, cq, qidx, w_idx, kidxT, ckvT, ckv, wuq, wukT, wuvp)


def _mixer_c(x, w_in, q_norm, kv_norm, w_uq, w_uk, w_uv):
    return _dsa_attention(*_dsa_proj(x, w_in, q_norm, kv_norm), w_uq, w_uk, w_uv)


def _moe_layer(layer, x1, route, cnt, routeT, p, w_gu, w_down, ln_g, ln_b, w_gate, w_proj, xs_init):
    N = x1.shape[0]
    M = N * TOPK_IN_GROUP
    flat_e = routeT[0:TOPK_IN_GROUP].astype(jnp.int32).T.reshape(M)
    rank = routeT[4:4 + TOPK_IN_GROUP].astype(jnp.int32).T.reshape(M)
    counts = cnt[0, N_GROUPS:N_GROUPS + N_EXPERTS].astype(jnp.int32)
    return _moe_experts(layer, x1, flat_e, rank, counts, route, p, w_gu, w_down, ln_g, ln_b, w_gate, w_proj,
                        xs_init)


MOE_BLK = 512
MOE_TM = 512
MOE_DMA_UNROLL = 8


def _dispatch_kernel(dest_ref, x_ref, xs_init_ref, xs_ref, sem):
    del xs_init_ref
    base = pl.program_id(0) * (TOPK_IN_GROUP * MOE_TM)

    def row_copy(t, k):
        d = dest_ref[base + TOPK_IN_GROUP * t + k]
        return pltpu.make_async_copy(x_ref.at[pl.ds(t, 1), :], xs_ref.at[pl.ds(d, 1), :], sem)

    def issue(t, c):
        for k in range(TOPK_IN_GROUP):
            row_copy(t, k).start(priority=k)
        return c

    def drain(t, c):
        for k in range(TOPK_IN_GROUP):
            row_copy(t, k).wait()
        return c

    lax.fori_loop(0, MOE_TM, issue, 0, unroll=MOE_DMA_UNROLL)
    lax.fori_loop(0, MOE_TM, drain, 0, unroll=MOE_DMA_UNROLL)


def _expert_kernel(be_ref, nu_ref, xs_ref, wgu_ref, wd_ref, ys_ref, wgu_bf, wd_bf):
    b = pl.program_id(0)
    F = wd_bf.shape[0]

    @pl.when(b < nu_ref[0])
    def _():
        e = be_ref[b]
        prev = be_ref[jnp.maximum(b - 1, 0)]

        @pl.when((b == 0) | (e != prev))
        def _():
            wgu_bf[...] = wgu_ref[0, 0].astype(jnp.bfloat16)
            wd_bf[...] = wd_ref[0, 0].astype(jnp.bfloat16)

        h = jnp.dot(xs_ref[...].astype(jnp.bfloat16), wgu_bf[...], preferred_element_type=jnp.float32)
        a = h[:, :F]
        act = a * jax.nn.sigmoid(a) * h[:, F:]
        ys_ref[...] = jnp.dot(act.astype(jnp.bfloat16), wd_bf[...], preferred_element_type=jnp.float32)

    @pl.when(b >= nu_ref[0])
    def _():
        ys_ref[...] = jnp.zeros(ys_ref.shape, ys_ref.dtype)


def _combine_kernel(dest_ref, ys_ref, route_ref, x1_ref, p_ref, lng_ref, lnb_ref, wg_ref, wp_ref,
                    o_ref, buf, sem):
    i = pl.program_id(0)
    n = pl.num_programs(0)

    def row_copy(step, slot, t, k):
        d = dest_ref[step * (TOPK_IN_GROUP * MOE_TM) + TOPK_IN_GROUP * t + k]
        return pltpu.make_async_copy(ys_ref.at[pl.ds(d, 1), :], buf.at[slot, k, pl.ds(t, 1), :], sem.at[slot])

    def issue(step, slot):
        def body(t, c):
            for k in range(TOPK_IN_GROUP):
                row_copy(step, slot, t, k).start(priority=k)
            return c
        lax.fori_loop(0, MOE_TM, body, 0, unroll=MOE_DMA_UNROLL)

    def drain(step, slot):
        def body(t, c):
            for k in range(TOPK_IN_GROUP):
                row_copy(step, slot, t, k).wait()
            return c
        lax.fori_loop(0, MOE_TM, body, 0, unroll=MOE_DMA_UNROLL)

    @pl.when(i == 0)
    def _():
        issue(0, 0)

    @pl.when(i + 1 < n)
    def _():
        issue(i + 1, (i + 1) % 2)

    slot = i % 2
    drain(i, slot)
    g = route_ref[...]
    m = g[:, 2:3] * buf[slot, 0] + g[:, 3:4] * buf[slot, 1]
    x2 = _ln_rows(ALPHA * x1_ref[...] + m, lng_ref[...], lnb_ref[...])
    o_ref[...] = x2 + jax.nn.sigmoid(_bdot(x2, wg_ref[...])) * _bdot(p_ref[...], wp_ref[...])


def _moe_experts(layer, xt, flat_e, rank, counts, route, p, w_gu, w_down, ln_g, ln_b, w_gate, w_proj, xs_init):
    N, D = xt.shape
    M = flat_e.shape[0]
    F = w_down.shape[2]
    BLK, TM = MOE_BLK, MOE_TM
    nb = -(-(M + N_EXPERTS * (BLK - 1)) // BLK)
    R = nb * BLK
    padded = ((counts + BLK - 1) // BLK) * BLK
    pend = jnp.cumsum(padded)
    pstart = pend - padded
    dest = (pstart[flat_e] + rank).astype(jnp.int32)
    nused = (pend[-1:] // BLK).astype(jnp.int32)
    blk_e = jnp.minimum(jnp.sum(pend[None, :] <= (jnp.arange(nb) * BLK)[:, None], axis=1),
                        N_EXPERTS - 1).astype(jnp.int32)

    xs = pl.pallas_call(
        _dispatch_kernel,
        grid_spec=pltpu.PrefetchScalarGridSpec(
            num_scalar_prefetch=1,
            grid=(N // TM,),
            in_specs=[pl.BlockSpec((TM, D), lambda i, d: (i, 0)),
                      pl.BlockSpec(memory_space=pl.ANY)],
            out_specs=pl.BlockSpec(memory_space=pl.ANY),
            scratch_shapes=[pltpu.SemaphoreType.DMA(())],
        ),
        out_shape=jax.ShapeDtypeStruct((R, D), jnp.float32),
        input_output_aliases={2: 0},
        compiler_params=pltpu.CompilerParams(dimension_semantics=("arbitrary",), vmem_limit_bytes=VMEM_LIMIT),
        name="moe_dispatch",
    )(dest, xt, jnp.zeros((R, D), jnp.float32) if xs_init is None else xs_init)

    def blk(b, be, nu):
        return jnp.minimum(b, nu[0] - 1)

    ys = pl.pallas_call(
        _expert_kernel,
        grid_spec=pltpu.PrefetchScalarGridSpec(
            num_scalar_prefetch=2,
            grid=(nb,),
            in_specs=[pl.BlockSpec((BLK, D), lambda b, be, nu: (blk(b, be, nu), 0)),
                      pl.BlockSpec((1, 1, D, 2 * F), lambda b, be, nu: (layer, be[blk(b, be, nu)], 0, 0)),
                      pl.BlockSpec((1, 1, F, D), lambda b, be, nu: (layer, be[blk(b, be, nu)], 0, 0))],
            out_specs=pl.BlockSpec((BLK, D), lambda b, be, nu: (b, 0)),
            scratch_shapes=[pltpu.VMEM((D, 2 * F), jnp.bfloat16), pltpu.VMEM((F, D), jnp.bfloat16)],
        ),
        out_shape=jax.ShapeDtypeStruct((R, D), jnp.float32),
        compiler_params=pltpu.CompilerParams(dimension_semantics=("arbitrary",), vmem_limit_bytes=VMEM_LIMIT),
        name="moe_experts",
    )(blk_e, nused, xs, w_gu, w_down)

    return xs, pl.pallas_call(
        _combine_kernel,
        grid_spec=pltpu.PrefetchScalarGridSpec(
            num_scalar_prefetch=1,
            grid=(N // TM,),
            in_specs=[pl.BlockSpec(memory_space=pl.ANY),
                      pl.BlockSpec((TM, ROUTE_W), lambda i, d: (i, 0)),
                      pl.BlockSpec((TM, D), lambda i, d: (i, 0)),
                      pl.BlockSpec((TM, p.shape[1]), lambda i, d: (i, 0)),
                      pl.BlockSpec((1, D), lambda i, d: (0, 0)),
                      pl.BlockSpec((1, D), lambda i, d: (0, 0)),
                      pl.BlockSpec((D, D), lambda i, d: (0, 0)),
                      pl.BlockSpec((p.shape[1], D), lambda i, d: (0, 0))],
            out_specs=pl.BlockSpec((TM, D), lambda i, d: (i, 0)),
            scratch_shapes=[pltpu.VMEM((2, TOPK_IN_GROUP, TM, D), jnp.float32),
                            pltpu.SemaphoreType.DMA((2,))],
        ),
        out_shape=jax.ShapeDtypeStruct((N, D), jnp.float32),
        compiler_params=pltpu.CompilerParams(dimension_semantics=("arbitrary",), vmem_limit_bytes=VMEM_LIMIT),
        name="moe_combine",
    )(dest, ys, route, xt, p, ln_g[None, :], ln_b[None, :],
      w_gate.astype(jnp.bfloat16), w_proj.astype(jnp.bfloat16))


def kernel(x, p, ab_w_in, ret_gn_g, pool_w, pool_scale, ab_w_out, c_w_in, c_q_norm, c_kv_norm, c_w_uq, c_w_uk, c_w_uv, c_w_out, ln1_g, ln1_b, ln2_g, ln2_b, moe_w_group, moe_b_group, moe_w_expert, moe_b_expert, moe_w_gu, moe_w_down, ple_w_gate, ple_w_proj):
    B, T, D = x.shape
    N = B * T
    xs_buf = None
    for i in range(DEPTH):
        j = i // 2
        wr, br = _router_weights(moe_w_group[i], moe_b_group[i], moe_w_expert[i], moe_b_expert[i])
        if i % 2 == 0:
            x1, route, cnt, routeT = _mixer_ab_layer(x, ab_w_in[j], ret_gn_g[j], pool_w[j], pool_scale[j],
                                                     ab_w_out[j], ln1_g[i], ln1_b[i], wr, br)
            x1, route = x1.reshape(N, D), route.reshape(N, ROUTE_W)
        else:
            o = _mixer_c(x, c_w_in[j], c_q_norm[j], c_kv_norm[j], c_w_uq[j], c_w_uk[j], c_w_uv[j])
            x1, route, cnt, routeT = _outproj_layer(o.reshape(N, -1), x.reshape(N, D), c_w_out[j],
                                                    ln1_g[i], ln1_b[i], wr, br)
        xs_buf, x = _moe_layer(i, x1, route, cnt, routeT, p[i].reshape(N, -1), moe_w_gu, moe_w_down,
                               ln2_g[i], ln2_b[i], ple_w_gate[i], ple_w_proj[i], xs_buf)
        x = x.reshape(B, T, D)
    return x
```

```python
import functools
import jax
import jax.numpy as jnp
from jax import lax
from jax.experimental import pallas as pl
from jax.experimental.pallas import tpu as pltpu

DEPTH = 2
RET_HEADS = 4
RET_DK = 128
RET_DV = 128
RET_CHUNK = 128
POOL_WINDOWS = (2, 4, 8, 16)
POOL_GROUP = 128
C_HEADS = 16
C_HEAD_DIM = 64
C_Q_RANK = 256
C_KV_RANK = 256
IDX_HEADS = 8
IDX_DIM = 64
IDX_TOPK_MAX = 256
Q_BLOCK = 128
N_GROUPS = 4
EXPERTS_PER_GROUP = 8
N_EXPERTS = N_GROUPS * EXPERTS_PER_GROUP
TOPK_IN_GROUP = 2
ALPHA = (2.0 * DEPTH) ** 0.25
LN_EPS = 1e-5

TOK_TM = 512
POOL_TAIL = 16
VMEM_LIMIT = 56 * 1024 * 1024
ROUTE_W = 128
ROUTE_ROWS = 8


def _bdot(a, b):
    return jnp.dot(a.astype(jnp.bfloat16), b.astype(jnp.bfloat16), preferred_element_type=jnp.float32)


def _ln_rows(v, g, b):
    mu = jnp.mean(v, axis=1, keepdims=True)
    d = v - mu
    var = jnp.mean(d * d, axis=1, keepdims=True)
    return d * lax.rsqrt(var + LN_EPS) * g + b


def _route_rows(x1, wr_ref, br_ref, tri_ref, run_scr, cnt_ref):
    f32 = jnp.float32
    logits = _bdot(x1, wr_ref[...]) + br_ref[...]
    lane = lax.broadcasted_iota(jnp.int32, logits.shape, 1)
    lane_f = lane.astype(f32)
    ninf = -jnp.inf
    gl = jnp.where(lane < N_GROUPS, logits, ninf)
    gmax = jnp.max(gl, axis=1, keepdims=True)
    grp = jnp.min(jnp.where(gl == gmax, lane_f, float(ROUTE_W)), axis=1, keepdims=True)
    g_w = 1.0 / jnp.sum(jnp.exp(gl - gmax), axis=1, keepdims=True)
    egrp = ((lane - N_GROUPS) >> 3).astype(f32)
    el = jnp.where(lane >= N_GROUPS, jnp.where(egrp == grp, logits, ninf), ninf)
    v1 = jnp.max(el, axis=1, keepdims=True)
    i1 = jnp.min(jnp.where(el == v1, lane_f, float(ROUTE_W)), axis=1, keepdims=True)
    el2 = jnp.where(lane_f == i1, ninf, el)
    v2 = jnp.max(el2, axis=1, keepdims=True)
    i2 = jnp.min(jnp.where(el2 == v2, lane_f, float(ROUTE_W)), axis=1, keepdims=True)
    e = jnp.exp(v2 - v1)
    den = 1.0 / (1.0 + e)
    oh1 = jnp.where(lane_f == i1, 1.0, 0.0)
    oh2 = jnp.where(lane_f == i2, 1.0, 0.0)
    oh = oh1 + oh2
    base = run_scr[0:1, :] + jnp.dot(tri_ref[...], oh.astype(jnp.bfloat16), preferred_element_type=f32)
    r1 = jnp.sum(oh1 * base, axis=1, keepdims=True)
    r2 = jnp.sum(oh2 * base, axis=1, keepdims=True)
    run_new = run_scr[0:1, :] + jnp.sum(oh, axis=0, keepdims=True)
    run_scr[0:1, :] = run_new
    cnt_ref[...] = jnp.broadcast_to(run_new, cnt_ref.shape)
    sel = lambda l, v, rest: jnp.where(lane == l, v, rest)
    return sel(0, i1 - N_GROUPS, sel(1, i2 - N_GROUPS, sel(2, g_w * den, sel(3, g_w * e * den,
                                                                             sel(4, r1, sel(5, r2, 0.0))))))


def _mixer_ab_kernel(cdec_ref, x_ref, win_ref, idec_ref, qdec_ref, kdec_ref, gng_ref, bandc_ref, bandp_ref,
                     pw_ref, psc_ref, wout_ref, lng_ref, lnb_ref, wr_ref, br_ref, tri_ref,
                     x1_ref, route_ref, cnt_ref, routeT_ref, stage_ref, r_scr, tail_scr, mixed_scr, run_scr):
    f32 = jnp.float32
    bf = jnp.bfloat16
    C = RET_CHUNK
    TM = x_ref.shape[1]
    nq = RET_HEADS * RET_DK
    nv = RET_HEADS * RET_DV
    ti = pl.program_id(1)

    @pl.when(ti == 0)
    def _():
        r_scr[...] = jnp.zeros(r_scr.shape, f32)
        tail_scr[...] = jnp.zeros(tail_scr.shape, f32)

    x = x_ref[0]
    xb = x.astype(bf)
    zq = jnp.dot(xb, win_ref[:, 0:nq], preferred_element_type=f32)
    zk = jnp.dot(xb, win_ref[:, nq:2 * nq], preferred_element_type=f32) * (RET_DK ** -0.5)
    zv = jnp.dot(xb, win_ref[:, 2 * nq:2 * nq + nv], preferred_element_type=f32)
    zg = jnp.dot(xb, win_ref[:, 2 * nq + nv:2 * nq + 2 * nv], preferred_element_type=f32)
    zu = jnp.dot(xb, win_ref[:, 2 * nq + 2 * nv:], preferred_element_type=f32)

    for h in range(RET_HEADS):
        hs = slice(h * RET_DK, (h + 1) * RET_DK)
        R = r_scr[h]
        for c in range(TM // C):
            cs = slice(c * C, (c + 1) * C)
            qc, kc, vc = zq[cs, hs], zk[cs, hs], zv[cs, hs]
            s = lax.dot_general(qc.astype(bf), kc.astype(bf), (((1,), (1,)), ((), ())),
                                preferred_element_type=f32) * idec_ref[h]
            o = _bdot(s, vc) + _bdot(qc * qdec_ref[h], R)
            R = R * cdec_ref[h] + lax.dot_general((kc * kdec_ref[h]).astype(bf), vc.astype(bf),
                                                  (((0,), (0,)), ((), ())), preferred_element_type=f32)
            mu = jnp.mean(o, axis=1, keepdims=True)
            d = o - mu
            var = jnp.mean(d * d, axis=1, keepdims=True)
            gate = zg[cs, hs]
            ret = d * lax.rsqrt(var + LN_EPS) * gng_ref[:, hs] * (gate * jax.nn.sigmoid(gate))
            mixed_scr[cs, hs] = ret.astype(bf)
        r_scr[h] = R

    t_glob = ti * TM + lax.broadcasted_iota(jnp.int32, (TM, POOL_GROUP), 0)
    tail = tail_scr[...]
    for gi, w in enumerate(POOL_WINDOWS):
        gs = slice(gi * POOL_GROUP, (gi + 1) * POOL_GROUP)
        u = zu[:, gs]
        u_hi = u.astype(bf)
        u_lo = (u - u_hi.astype(f32)).astype(bf)
        tl = tail[:, gs]
        t_hi = tl.astype(bf)
        t_lo = (tl - t_hi.astype(f32)).astype(bf)
        wsum = (jnp.dot(bandc_ref[gi], u_hi, preferred_element_type=f32)
                + jnp.dot(bandc_ref[gi], u_lo, preferred_element_type=f32)
                + jnp.dot(bandp_ref[gi], t_hi, preferred_element_type=f32)
                + jnp.dot(bandp_ref[gi], t_lo, preferred_element_type=f32))
        cnt = jnp.minimum(t_glob + 1, w).astype(f32)
        dpool = wsum / cnt - u
        y = _bdot(dpool, pw_ref[gi]) * psc_ref[:, gs]
        mixed_scr[:, nv + gi * POOL_GROUP:nv + (gi + 1) * POOL_GROUP] = y.astype(bf)
    tail_scr[...] = zu[TM - POOL_TAIL:, :]

    hmix = jnp.dot(mixed_scr[...], wout_ref[...], preferred_element_type=f32)
    x1 = _ln_rows(ALPHA * x + hmix, lng_ref[...], lnb_ref[...])
    x1_ref[0] = x1
    @pl.when((pl.program_id(0) == 0) & (ti == 0))
    def _():
        run_scr[...] = jnp.zeros(run_scr.shape, f32)

    route = _route_rows(x1, wr_ref, br_ref, tri_ref, run_scr, cnt_ref)
    route_ref[0] = route
    routeT_ref[...] = route.T[0:ROUTE_ROWS, :]
    stage_ref[...] = jnp.zeros(stage_ref.shape, f32)


def _const_spec(shape):
    return pl.BlockSpec(shape, lambda b, t: (0,) * len(shape))


def _router_weights(w_group, b_group, w_expert, b_expert):
    D = w_group.shape[0]
    pad = ROUTE_W - N_GROUPS - N_EXPERTS
    wr = jnp.concatenate([w_group, w_expert, jnp.zeros((D, pad), jnp.float32)], axis=1).astype(jnp.bfloat16)
    br = jnp.concatenate([b_group, b_expert, jnp.zeros((pad,), jnp.float32)])[None, :]
    return wr, br


def _staging_rows(n_tokens):
    pairs = n_tokens * TOPK_IN_GROUP
    return -(-(pairs + N_EXPERTS * (MOE_BLK - 1)) // MOE_BLK) * MOE_BLK


def _mixer_ab_layer(x, w_in, gn_g, pool_w, pool_scale, w_out, ln_g, ln_b, wr, br):
    B, T, D = x.shape
    TM = TOK_TM
    steps = B * (T // TM)
    stage_rows = _staging_rows(B * T)
    assert stage_rows % (8 * steps) == 0, "staging buffer must split into sublane-aligned blocks per grid step"
    C = RET_CHUNK
    H = RET_HEADS
    bf = jnp.bfloat16
    log_g = jnp.log1p(-jnp.exp2(-5.0 - jnp.arange(H, dtype=jnp.float32)))
    i = jnp.arange(C, dtype=jnp.float32)
    diff = i[:, None] - i[None, :]
    idec = jnp.where(diff >= 0, jnp.exp(log_g[:, None, None] * jnp.maximum(diff, 0.0)), 0.0)
    qdec = jnp.broadcast_to(jnp.exp(log_g[:, None] * (i + 1.0))[:, :, None], (H, C, RET_DK))
    kdec = jnp.broadcast_to(jnp.exp(log_g[:, None] * (C - 1.0 - i))[:, :, None], (H, C, RET_DK))
    cdec = jnp.exp(log_g * C)
    r = jnp.arange(TM)[:, None]
    bandc = jnp.stack([((r - jnp.arange(TM)[None, :] >= 0) & (r - jnp.arange(TM)[None, :] < w))
                       for w in POOL_WINDOWS]).astype(bf)
    jp = jnp.arange(POOL_TAIL)[None, :] - POOL_TAIL
    bandp = jnp.stack([((r - jp) < w) for w in POOL_WINDOWS]).astype(bf)
    nG = len(POOL_WINDOWS)
    return pl.pallas_call(
        _mixer_ab_kernel,
        grid=(B, T // TM),
        in_specs=[pl.BlockSpec(memory_space=pltpu.SMEM),
                  pl.BlockSpec((1, TM, D), lambda b, t: (b, t, 0)),
                  _const_spec(w_in.shape), _const_spec((H, C, C)), _const_spec((H, C, RET_DK)),
                  _const_spec((H, C, RET_DK)), _const_spec((1, H * RET_DV)),
                  _const_spec((nG, TM, TM)), _const_spec((nG, TM, POOL_TAIL)),
                  _const_spec((nG, POOL_GROUP, POOL_GROUP)), _const_spec((1, nG * POOL_GROUP)),
                  _const_spec(w_out.shape), _const_spec((1, D)), _const_spec((1, D)),
                  _const_spec((D, ROUTE_W)), _const_spec((1, ROUTE_W)), _const_spec((TM, TM))],
        out_specs=[pl.BlockSpec((1, TM, D), lambda b, t: (b, t, 0)),
                   pl.BlockSpec((1, TM, ROUTE_W), lambda b, t: (b, t, 0)),
                   _const_spec((8, ROUTE_W)),
                   pl.BlockSpec((ROUTE_ROWS, TM), lambda b, t: (0, b * (T // TM) + t)),
                   pl.BlockSpec((stage_rows // steps, D), lambda b, t: (b * (T // TM) + t, 0))],
        out_shape=[jax.ShapeDtypeStruct((B, T, D), jnp.float32),
                   jax.ShapeDtypeStruct((B, T, ROUTE_W), jnp.float32),
                   jax.ShapeDtypeStruct((8, ROUTE_W), jnp.float32),
                   jax.ShapeDtypeStruct((ROUTE_ROWS, B * T), jnp.float32),
                   jax.ShapeDtypeStruct((stage_rows, D), jnp.float32)],
        scratch_shapes=[pltpu.VMEM((H, RET_DK, RET_DV), jnp.float32),
                        pltpu.VMEM((POOL_TAIL, nG * POOL_GROUP), jnp.float32),
                        pltpu.VMEM((TM, w_out.shape[0]), bf),
                        pltpu.VMEM((8, ROUTE_W), jnp.float32)],
        compiler_params=pltpu.CompilerParams(dimension_semantics=("arbitrary", "arbitrary"),
                                             vmem_limit_bytes=VMEM_LIMIT),
        name="mixer_ab_layer",
    )(cdec, x, w_in.astype(bf), idec, qdec, kdec, gn_g[None, :], bandc, bandp,
      pool_w.astype(bf), pool_scale[None, :], w_out.astype(bf), ln_g[None, :], ln_b[None, :], wr, br,
      _strict_lower_tri(TM))


def _outproj_kernel(o_ref, x_ref, wout_ref, lng_ref, lnb_ref, wr_ref, br_ref, tri_ref,
                    x1_ref, route_ref, cnt_ref, routeT_ref, run_scr):
    @pl.when(pl.program_id(0) == 0)
    def _():
        run_scr[...] = jnp.zeros(run_scr.shape, jnp.float32)

    h = _bdot(o_ref[...], wout_ref[...])
    x1 = _ln_rows(ALPHA * x_ref[...] + h, lng_ref[...], lnb_ref[...])
    x1_ref[...] = x1
    route = _route_rows(x1, wr_ref, br_ref, tri_ref, run_scr, cnt_ref)
    route_ref[...] = route
    routeT_ref[...] = route.T[0:ROUTE_ROWS, :]


def _strict_lower_tri(n):
    r = jnp.arange(n)
    return (r[:, None] > r[None, :]).astype(jnp.bfloat16)


def _outproj_layer(o, x, w_out, ln_g, ln_b, wr, br):
    N, D = x.shape
    TM = TOK_TM
    cs = lambda shape: pl.BlockSpec(shape, lambda i: (0,) * len(shape))
    return pl.pallas_call(
        _outproj_kernel,
        grid=(N // TM,),
        in_specs=[pl.BlockSpec((TM, o.shape[1]), lambda i: (i, 0)),
                  pl.BlockSpec((TM, D), lambda i: (i, 0)),
                  cs(w_out.shape), cs((1, D)), cs((1, D)), cs((D, ROUTE_W)), cs((1, ROUTE_W)),
                  cs((TM, TM))],
        out_specs=[pl.BlockSpec((TM, D), lambda i: (i, 0)),
                   pl.BlockSpec((TM, ROUTE_W), lambda i: (i, 0)),
                   cs((8, ROUTE_W)),
                   pl.BlockSpec((ROUTE_ROWS, TM), lambda i: (0, i))],
        out_shape=[jax.ShapeDtypeStruct((N, D), jnp.float32),
                   jax.ShapeDtypeStruct((N, ROUTE_W), jnp.float32),
                   jax.ShapeDtypeStruct((8, ROUTE_W), jnp.float32),
                   jax.ShapeDtypeStruct((ROUTE_ROWS, N), jnp.float32)],
        scratch_shapes=[pltpu.VMEM((8, ROUTE_W), jnp.float32)],
        compiler_params=pltpu.CompilerParams(dimension_semantics=("arbitrary",),
                                             vmem_limit_bytes=VMEM_LIMIT),
        name="outproj_layer",
    )(o, x, w_out.astype(jnp.bfloat16), ln_g[None, :], ln_b[None, :], wr, br, _strict_lower_tri(TM))


LANES = 128
QB = Q_BLOCK
DSA_TK = 512
DSA_HG = 4
DSA_VALUE_STEPS = 24
DSA_ROUNDS_PER_TEST = 4
DSA_COARSE_STEPS = 16
NEG_BIG = -1e30
KEY_LO0 = -2139095040
LOG2E = 1.4426950408889634
BIG_IDX = 1 << 30


def _rep(x, n, axis):
    return jnp.concatenate([x] * n, axis=axis) if n > 1 else x


def _sum_lane_chunks(v, width):
    out = v[:, 0:LANES]
    for c in range(1, width // LANES):
        out = out + v[:, c * LANES:(c + 1) * LANES]
    return out


def _dsa_kernel(slopes_ref, cq_ref, qidx_ref, widx_ref, kidxT_ref, ckvT_ref, ckv_ref,
                wuq_ref, wukT_ref, wuvp_ref, o_ref,
                keys_scr, wb_scr, q_scr, a_scr, m_scr, l_scr, acc_scr,
                th_scr, c_scr, *, topk, seq_len):
    f32 = jnp.float32
    TK = DSA_TK
    NCH = TK // LANES
    H = C_HEADS
    qb = pl.program_id(1)
    t0 = qb * QB
    n_kt = (t0 + QB - 1) // TK + 1
    row = lax.broadcasted_iota(jnp.int32, (QB, TK), 0)
    col = lax.broadcasted_iota(jnp.int32, (QB, TK), 1)
    kf = float(topk)

    for h in range(IDX_HEADS):
        wb_scr[h] = jnp.broadcast_to(widx_ref[0, :, h:h + 1], (QB, LANES))

    def idx_body(j, carry):
        kT = kidxT_ref[0, j]
        s_all = jnp.dot(qidx_ref[0].reshape(IDX_HEADS * QB, IDX_DIM), kT, preferred_element_type=f32)
        acc = jnp.zeros((QB, TK), f32)
        for h in range(IDX_HEADS):
            acc = acc + _rep(wb_scr[h], NCH, axis=1) * jnp.maximum(s_all[h * QB:(h + 1) * QB, :], 0.0)
        causal = (j * TK + col) <= (t0 + row)
        sc = jnp.where(causal, acc, -jnp.inf)
        b = pltpu.bitcast(sc, jnp.int32)
        keys_scr[j] = b ^ ((b >> 31) & 0x7FFFFFFF)
        vmax, vmin = carry
        sc_hi = jnp.where(causal, acc, jnp.inf)
        for c in range(NCH):
            vmax = jnp.maximum(vmax, sc[:, c * LANES:(c + 1) * LANES])
            vmin = jnp.minimum(vmin, sc_hi[:, c * LANES:(c + 1) * LANES])
        return vmax, vmin

    vmax, vmin = lax.fori_loop(0, n_kt, idx_body, (jnp.full((QB, LANES), -jnp.inf, f32),
                                                   jnp.full((QB, LANES), jnp.inf, f32)))

    def to_key(v):
        b = pltpu.bitcast(v, jnp.int32)
        return b ^ ((b >> 31) & 0x7FFFFFFF)

    def from_key(k):
        return pltpu.bitcast(k ^ ((k >> 31) & 0x7FFFFFFF), f32)

    @pl.when(n_kt < keys_scr.shape[0])
    def _():
        keys_scr[n_kt] = jnp.full((QB, TK), KEY_LO0 - 1, jnp.int32)

    def count_where(ones_fn):
        def body(j2, c):
            j = 2 * j2
            return (c + _sum_lane_chunks(ones_fn(j, keys_scr[j]), TK)
                    + _sum_lane_chunks(ones_fn(j + 1, keys_scr[j + 1]), TK))
        c = lax.fori_loop(0, (n_kt + 1) // 2, body, jnp.zeros((QB, LANES), f32))
        return jnp.broadcast_to(jnp.sum(c, axis=1, keepdims=True), (QB, LANES))

    n_valid = (t0 + 1 + lax.broadcasted_iota(jnp.int32, (QB, LANES), 0)).astype(f32)
    small = n_valid <= kf
    kmax = to_key(jnp.broadcast_to(jnp.max(vmax, axis=1, keepdims=True), (QB, LANES)))
    kmin = to_key(jnp.broadcast_to(jnp.min(vmin, axis=1, keepdims=True), (QB, LANES)))
    lo0 = jnp.where(small, KEY_LO0, kmin)
    hi0 = jnp.where(small, KEY_LO0 + 1, kmax + 1)

    def row_done(lo, hi, clo):
        return jnp.where(clo == kf, 1.0, jnp.where(hi == lo + 1, 1.0, 0.0))

    def search_round(mid_fn, lo, hi, clo, chi):
        mid = jnp.minimum(jnp.maximum(mid_fn(lo, hi), lo + 1), hi - 1)
        midb = _rep(mid, NCH, axis=1)
        cnt = count_where(lambda j, k: jnp.where(k >= midb, 1.0, 0.0))
        ge = jnp.where(small, kf, cnt) >= kf
        return jnp.where(ge, mid, lo), jnp.where(ge, hi, mid), jnp.where(ge, cnt, clo), jnp.where(ge, chi, cnt)

    def search(mid_fn, rounds_per_test, max_rounds, st):
        def cond(s):
            return (s[1] > 0) & (s[0] < max_rounds)

        def body(s):
            it, _, lo, hi, clo, chi = s
            for _ in range(rounds_per_test):
                lo, hi, clo, chi = search_round(mid_fn, lo, hi, clo, chi)
            pending = (jnp.min(row_done(lo, hi, clo)) < 0.5).astype(jnp.int32)
            return it + rounds_per_test, pending, lo, hi, clo, chi

        return lax.while_loop(cond, body, st)

    value_mid = lambda lo, hi: to_key(0.5 * from_key(lo) + 0.5 * from_key(hi - 1))
    key_mid = lambda lo, hi: (lo & hi) + ((lo ^ hi) >> 1)
    pending0 = (jnp.min(row_done(lo0, hi0, n_valid)) < 0.5).astype(jnp.int32)
    st = (jnp.int32(0), pending0, lo0, hi0, n_valid, jnp.zeros((QB, LANES), f32))
    st = search(value_mid, DSA_ROUNDS_PER_TEST, DSA_COARSE_STEPS, st)
    st = search(value_mid, DSA_ROUNDS_PER_TEST // 2, DSA_VALUE_STEPS, st)
    st = search(key_mid, DSA_ROUNDS_PER_TEST, 32 + DSA_ROUNDS_PER_TEST, (jnp.int32(0),) + st[1:])
    _, _, theta, _, n_ge, n_gt = st
    th_scr[...] = theta
    thb = _rep(theta, NCH, axis=1)

    need = kf - n_gt
    excess = n_ge > kf
    c_scr[...] = jnp.full((QB, LANES), BIG_IDX, jnp.int32)

    @pl.when(jnp.max(jnp.where(excess, 1.0, 0.0)) > 0.0)
    def _():
        def tie_body(i, carry):
            lo, hi = carry
            mid = (lo + hi) >> 1
            midb = _rep(mid, NCH, axis=1)
            cnt = count_where(lambda j, k: jnp.where(
                k == thb, jnp.where((j * TK + col) <= midb, 1.0, 0.0), 0.0))
            ok = cnt >= need
            return jnp.where(ok, lo, mid + 1), jnp.where(ok, mid, hi)
        nbits = max(1, (seq_len - 1).bit_length())
        lo, _ = lax.fori_loop(0, nbits, tie_body,
                              (jnp.zeros((QB, LANES), jnp.int32),
                               jnp.full((QB, LANES), seq_len - 1, jnp.int32)))
        c_scr[...] = jnp.where(excess, lo, BIG_IDX)

    qh_all = jnp.dot(cq_ref[0], wuq_ref[...], preferred_element_type=f32).astype(jnp.bfloat16)
    for h in range(H):
        qh = qh_all[:, h * C_HEAD_DIM:(h + 1) * C_HEAD_DIM]
        qa = jnp.dot(qh, wukT_ref[h], preferred_element_type=f32) * (C_HEAD_DIM ** -0.5 * LOG2E)
        q_scr[h * QB:(h + 1) * QB, :] = qa.astype(jnp.bfloat16)

    m_scr[...] = jnp.full(m_scr.shape, NEG_BIG, f32)
    l_scr[...] = jnp.zeros(l_scr.shape, f32)
    acc_scr[...] = jnp.zeros(acc_scr.shape, f32)

    def att_body(j, carry):
        kT = ckvT_ref[0, j]
        kv = ckv_ref[0, j]
        k = keys_scr[j]
        thb_ = _rep(th_scr[...], NCH, axis=1)
        cb_ = _rep(c_scr[...], NCH, axis=1)
        spos = j * TK + col
        keep_tie = jnp.where(spos <= cb_, 0.0, NEG_BIG)
        a_scr[...] = jnp.where(k > thb_, 0.0, jnp.where(k == thb_, keep_tie, NEG_BIG))
        srel8 = (j * TK - t0 + lax.broadcasted_iota(jnp.int32, (8, TK), 1)).astype(f32)

        for g in range(H // DSA_HG):
            g0 = g * DSA_HG * QB
            lg = jnp.dot(q_scr[g0:g0 + DSA_HG * QB, :], kT, preferred_element_type=f32)
            ps, alphas = [], []
            for u in range(DSA_HG):
                h = g * DSA_HG + u
                r = slice(h * QB, (h + 1) * QB)
                bh = _rep(slopes_ref[h] * srel8, QB // 8, axis=0)
                l = lg[u * QB:(u + 1) * QB, :] + a_scr[...] + bh
                m_old = m_scr[r, :]
                m_new = jnp.maximum(m_old, jnp.max(l, axis=1, keepdims=True))
                p = jnp.exp2(l - _rep(m_new, NCH, axis=1))
                alpha = jnp.exp2(m_old - m_new)
                rs = jnp.sum(_sum_lane_chunks(p, TK), axis=1, keepdims=True)
                l_scr[r, :] = alpha * l_scr[r, :] + rs
                m_scr[r, :] = m_new
                ps.append(p.astype(jnp.bfloat16))
                alphas.append(_rep(alpha, C_KV_RANK // LANES, axis=1))
            pv = jnp.dot(jnp.concatenate(ps, axis=0), kv, preferred_element_type=f32)
            acc_scr[g0:g0 + DSA_HG * QB, :] = (acc_scr[g0:g0 + DSA_HG * QB, :]
                                               * jnp.concatenate(alphas, axis=0) + pv)
        return carry

    lax.fori_loop(0, n_kt, att_body, 0)

    for g in range(H // 2):
        parts = []
        for hh in (2 * g, 2 * g + 1):
            inv = 1.0 / l_scr[hh * QB:(hh + 1) * QB, :]
            ol = acc_scr[hh * QB:(hh + 1) * QB, :] * _rep(inv, C_KV_RANK // LANES, axis=1)
            parts.append(jnp.dot(ol.astype(jnp.bfloat16), wuvp_ref[hh], preferred_element_type=f32))
        o_ref[0, :, g * LANES:(g + 1) * LANES] = parts[0] + parts[1]


def _dsa_proj_kernel(x_ref, wq_ref, wkv_ref, wkvT_ref, wqi_ref, wkiT_ref, wwi_ref, qn_ref, kvn_ref, kvnT_ref,
                     cq_ref, ckv_ref, ckvT_ref, qidx_ref, kidxT_ref, widx_ref):
    f32 = jnp.float32
    bf = jnp.bfloat16
    xb = x_ref[...].astype(bf)

    def rms(z, g):
        return z * lax.rsqrt(jnp.mean(z * z, axis=1, keepdims=True) + 1e-6) * g

    cq_ref[...] = rms(jnp.dot(xb, wq_ref[...], preferred_element_type=f32), qn_ref[...]).astype(bf)
    ckv = rms(jnp.dot(xb, wkv_ref[...], preferred_element_type=f32), kvn_ref[...])
    ckv_ref[0] = ckv.astype(bf)
    nt = (((1,), (1,)), ((), ()))
    zT = lax.dot_general(wkvT_ref[...], xb, nt, preferred_element_type=f32)
    ssq = jnp.sum(zT * zT, axis=0, keepdims=True) * (1.0 / C_KV_RANK)
    ckvT_ref[0] = (zT * lax.rsqrt(ssq + 1e-6) * kvnT_ref[...]).astype(bf)
    zqi = jnp.dot(xb, wqi_ref[...], preferred_element_type=f32)
    for h in range(IDX_HEADS):
        qidx_ref[0, h] = zqi[:, h * IDX_DIM:(h + 1) * IDX_DIM].astype(bf)
    kidxT_ref[0] = lax.dot_general(wkiT_ref[...], xb, nt, preferred_element_type=f32).astype(bf)
    widx_ref[...] = jnp.dot(xb, wwi_ref[...], preferred_element_type=f32)[:, 0:IDX_HEADS]


def _dsa_proj(x, w_in, q_norm, kv_norm):
    B, T, D = x.shape
    N = B * T
    TK = DSA_TK
    NT = T // TK
    bf = jnp.bfloat16
    s1 = C_Q_RANK
    s2 = s1 + C_KV_RANK
    s3 = s2 + IDX_HEADS * IDX_DIM
    s4 = s3 + IDX_DIM
    wq = w_in[:, :s1].astype(bf)
    wkv = w_in[:, s1:s2].astype(bf)
    wqi = w_in[:, s2:s3].astype(bf)
    wkvT = wkv.T
    wkiT = w_in[:, s3:s4].T.astype(bf)
    wwi = jnp.pad(w_in[:, s4:], ((0, 0), (0, LANES - IDX_HEADS))).astype(bf)
    cs = lambda shape: pl.BlockSpec(shape, lambda i: (0,) * len(shape))
    cq, ckv, ckvT, qidx, kidxT, widx = pl.pallas_call(
        _dsa_proj_kernel,
        grid=(N // TK,),
        in_specs=[pl.BlockSpec((TK, D), lambda i: (i, 0)),
                  cs(wq.shape), cs(wkv.shape), cs(wkvT.shape), cs(wqi.shape), cs(wkiT.shape), cs(wwi.shape),
                  cs((1, C_Q_RANK)), cs((1, C_KV_RANK)), cs((C_KV_RANK, 1))],
        out_specs=[pl.BlockSpec((TK, C_Q_RANK), lambda i: (i, 0)),
                   pl.BlockSpec((1, TK, C_KV_RANK), lambda i: (i, 0, 0)),
                   pl.BlockSpec((1, C_KV_RANK, TK), lambda i: (i, 0, 0)),
                   pl.BlockSpec((1, IDX_HEADS, TK, IDX_DIM), lambda i: (i // NT, 0, i % NT, 0)),
                   pl.BlockSpec((1, IDX_DIM, TK), lambda i: (i, 0, 0)),
                   pl.BlockSpec((TK, IDX_HEADS), lambda i: (i, 0))],
        out_shape=[jax.ShapeDtypeStruct((N, C_Q_RANK), bf),
                   jax.ShapeDtypeStruct((B * NT, TK, C_KV_RANK), bf),
                   jax.ShapeDtypeStruct((B * NT, C_KV_RANK, TK), bf),
                   jax.ShapeDtypeStruct((B, IDX_HEADS, T, IDX_DIM), bf),
                   jax.ShapeDtypeStruct((B * NT, IDX_DIM, TK), bf),
                   jax.ShapeDtypeStruct((N, IDX_HEADS), jnp.float32)],
        compiler_params=pltpu.CompilerParams(dimension_semantics=("arbitrary",),
                                             vmem_limit_bytes=VMEM_LIMIT),
        name="dsa_proj",
    )(x.reshape(N, D), wq, wkv, wkvT, wqi, wkiT, wwi, q_norm[None, :], kv_norm[None, :], kv_norm[:, None])
    return (cq.reshape(B, T, C_Q_RANK), qidx, widx.reshape(B, T, IDX_HEADS),
            kidxT.reshape(B, NT, IDX_DIM, TK), ckvT.reshape(B, NT, C_KV_RANK, TK),
            ckv.reshape(B, NT, TK, C_KV_RANK))


def _dsa_attention(cq, qidx, w_idx, kidxT, ckvT, ckv, w_uq, w_uk, w_uv):
    B, T, _ = cq.shape
    topk = min(IDX_TOPK_MAX, T // 4)
    TK = DSA_TK
    NT = T // TK
    assert T % (2 * TK) == 0, "counting passes walk key tiles in pairs"
    H = C_HEADS
    bf = jnp.bfloat16
    slopes2 = jnp.exp2(-8.0 * jnp.arange(1, H + 1, dtype=jnp.float32) / H) * LOG2E
    wuq = w_uq.astype(bf)
    wukT = w_uk.transpose(1, 2, 0).astype(bf)
    wuv = w_uv.transpose(1, 0, 2)
    zeros = jnp.zeros_like(wuv)
    even = (jnp.arange(H) % 2 == 0)[:, None, None]
    wuvp = jnp.where(even, jnp.concatenate([wuv, zeros], -1), jnp.concatenate([zeros, wuv], -1)).astype(bf)

    kern = functools.partial(_dsa_kernel, topk=topk, seq_len=T)
    full = lambda shape: pl.BlockSpec(shape, lambda b, q, *_: (0,) * len(shape))
    grid_spec = pltpu.PrefetchScalarGridSpec(
        num_scalar_prefetch=0,
        grid=(B, T // QB),
        in_specs=[
            pl.BlockSpec(memory_space=pltpu.SMEM),
            pl.BlockSpec((1, QB, C_Q_RANK), lambda b, q: (b, q, 0)),
            pl.BlockSpec((1, IDX_HEADS, QB, IDX_DIM), lambda b, q: (b, 0, q, 0)),
            pl.BlockSpec((1, QB, IDX_HEADS), lambda b, q: (b, q, 0)),
            pl.BlockSpec((1, NT, IDX_DIM, TK), lambda b, q: (b, 0, 0, 0)),
            pl.BlockSpec((1, NT, C_KV_RANK, TK), lambda b, q: (b, 0, 0, 0)),
            pl.BlockSpec((1, NT, TK, C_KV_RANK), lambda b, q: (b, 0, 0, 0)),
            full((C_Q_RANK, H * C_HEAD_DIM)),
            full((H, C_HEAD_DIM, C_KV_RANK)),
            full((H, C_KV_RANK, LANES)),
        ],
        out_specs=pl.BlockSpec((1, QB, H * C_HEAD_DIM), lambda b, q: (b, q, 0)),
        scratch_shapes=[
            pltpu.VMEM((NT, QB, TK), jnp.int32),
            pltpu.VMEM((IDX_HEADS, QB, LANES), jnp.float32),
            pltpu.VMEM((H * QB, C_KV_RANK), bf),
            pltpu.VMEM((QB, TK), jnp.float32),
            pltpu.VMEM((H * QB, LANES), jnp.float32),
            pltpu.VMEM((H * QB, LANES), jnp.float32),
            pltpu.VMEM((H * QB, C_KV_RANK), jnp.float32),
            pltpu.VMEM((QB, LANES), jnp.int32),
            pltpu.VMEM((QB, LANES), jnp.int32),
        ],
    )
    return pl.pallas_call(
        kern,
        grid_spec=grid_spec,
        out_shape=jax.ShapeDtypeStruct((B, T, H * C_HEAD_DIM), jnp.float32),
        compiler_params=pltpu.CompilerParams(dimension_semantics=("arbitrary", "arbitrary"),
                                             vmem_limit_bytes=VMEM_LIMIT),
        name="dsa_attention",
    )(slopes2, cq, qidx, w_idx, kidxT, ckvT, ckv, wuq, wukT, wuvp)


def _mixer_c(x, w_in, q_norm, kv_norm, w_uq, w_uk, w_uv):
    return _dsa_attention(*_dsa_proj(x, w_in, q_norm, kv_norm), w_uq, w_uk, w_uv)


def _moe_layer(layer, x1, route, cnt, routeT, p, w_gu, w_down, ln_g, ln_b, w_gate, w_proj, xs_init):
    N = x1.shape[0]
    M = N * TOPK_IN_GROUP
    flat_e = routeT[0:TOPK_IN_GROUP].astype(jnp.int32).T.reshape(M)
    rank = routeT[4:4 + TOPK_IN_GROUP].astype(jnp.int32).T.reshape(M)
    counts = cnt[0, N_GROUPS:N_GROUPS + N_EXPERTS].astype(jnp.int32)
    return _moe_experts(layer, x1, flat_e, rank, counts, route, p, w_gu, w_down, ln_g, ln_b, w_gate, w_proj,
                        xs_init)


MOE_BLK = 512
MOE_TM = 512
MOE_DMA_UNROLL = 8


def _dispatch_kernel(dest_ref, x_ref, xs_init_ref, xs_ref, sem):
    del xs_init_ref
    base = pl.program_id(0) * (TOPK_IN_GROUP * MOE_TM)

    def row_copy(t, k):
        d = dest_ref[base + TOPK_IN_GROUP * t + k]
        return pltpu.make_async_copy(x_ref.at[pl.ds(t, 1), :], xs_ref.at[pl.ds(d, 1), :], sem)

    def issue(t, c):
        for k in range(TOPK_IN_GROUP):
            row_copy(t, k).start(priority=k)
        return c

    def drain(t, c):
        for k in range(TOPK_IN_GROUP):
            row_copy(t, k).wait()
        return c

    lax.fori_loop(0, MOE_TM, issue, 0, unroll=MOE_DMA_UNROLL)
    lax.fori_loop(0, MOE_TM, drain, 0, unroll=MOE_DMA_UNROLL)


def _expert_kernel(be_ref, nu_ref, xs_ref, wgu_ref, wd_ref, ys_ref, wgu_bf, wd_bf):
    b = pl.program_id(0)
    F = wd_bf.shape[0]

    @pl.when(b < nu_ref[0])
    def _():
        e = be_ref[b]
        prev = be_ref[jnp.maximum(b - 1, 0)]

        @pl.when((b == 0) | (e != prev))
        def _():
            wgu_bf[...] = wgu_ref[0, 0].astype(jnp.bfloat16)
            wd_bf[...] = wd_ref[0, 0].astype(jnp.bfloat16)

        h = jnp.dot(xs_ref[...].astype(jnp.bfloat16), wgu_bf[...], preferred_element_type=jnp.float32)
        a = h[:, :F]
        act = a * jax.nn.sigmoid(a) * h[:, F:]
        ys_ref[...] = jnp.dot(act.astype(jnp.bfloat16), wd_bf[...], preferred_element_type=jnp.float32)

    @pl.when(b >= nu_ref[0])
    def _():
        ys_ref[...] = jnp.zeros(ys_ref.shape, ys_ref.dtype)


def _combine_kernel(dest_ref, ys_ref, route_ref, x1_ref, p_ref, lng_ref, lnb_ref, wg_ref, wp_ref,
                    o_ref, buf, sem):
    i = pl.program_id(0)
    n = pl.num_programs(0)

    def row_copy(step, slot, t, k):
        d = dest_ref[step * (TOPK_IN_GROUP * MOE_TM) + TOPK_IN_GROUP * t + k]
        return pltpu.make_async_copy(ys_ref.at[pl.ds(d, 1), :], buf.at[slot, k, pl.ds(t, 1), :], sem.at[slot])

    def issue(step, slot):
        def body(t, c):
            for k in range(TOPK_IN_GROUP):
                row_copy(step, slot, t, k).start(priority=k)
            return c
        lax.fori_loop(0, MOE_TM, body, 0, unroll=MOE_DMA_UNROLL)

    def drain(step, slot):
        def body(t, c):
            for k in range(TOPK_IN_GROUP):
                row_copy(step, slot, t, k).wait()
            return c
        lax.fori_loop(0, MOE_TM, body, 0, unroll=MOE_DMA_UNROLL)

    @pl.when(i == 0)
    def _():
        issue(0, 0)

    @pl.when(i + 1 < n)
    def _():
        issue(i + 1, (i + 1) % 2)

    slot = i % 2
    drain(i, slot)
    g = route_ref[...]
    m = g[:, 2:3] * buf[slot, 0] + g[:, 3:4] * buf[slot, 1]
    x2 = _ln_rows(ALPHA * x1_ref[...] + m, lng_ref[...], lnb_ref[...])
    o_ref[...] = x2 + jax.nn.sigmoid(_bdot(x2, wg_ref[...])) * _bdot(p_ref[...], wp_ref[...])


def _moe_experts(layer, xt, flat_e, rank, counts, route, p, w_gu, w_down, ln_g, ln_b, w_gate, w_proj, xs_init):
    N, D = xt.shape
    M = flat_e.shape[0]
    F = w_down.shape[2]
    BLK, TM = MOE_BLK, MOE_TM
    R = _staging_rows(N)
    nb = R // BLK
    padded = ((counts + BLK - 1) // BLK) * BLK
    pend = jnp.cumsum(padded)
    pstart = pend - padded
    dest = (pstart[flat_e] + rank).astype(jnp.int32)
    nused = (pend[-1:] // BLK).astype(jnp.int32)
    blk_e = jnp.minimum(jnp.sum(pend[None, :] <= (jnp.arange(nb) * BLK)[:, None], axis=1),
                        N_EXPERTS - 1).astype(jnp.int32)

    xs = pl.pallas_call(
        _dispatch_kernel,
        grid_spec=pltpu.PrefetchScalarGridSpec(
            num_scalar_prefetch=1,
            grid=(N // TM,),
            in_specs=[pl.BlockSpec((TM, D), lambda i, d: (i, 0)),
                      pl.BlockSpec(memory_space=pl.ANY)],
            out_specs=pl.BlockSpec(memory_space=pl.ANY),
            scratch_shapes=[pltpu.SemaphoreType.DMA(())],
        ),
        out_shape=jax.ShapeDtypeStruct((R, D), jnp.float32),
        input_output_aliases={2: 0},
        compiler_params=pltpu.CompilerParams(dimension_semantics=("arbitrary",), vmem_limit_bytes=VMEM_LIMIT),
        name="moe_dispatch",
    )(dest, xt, jnp.zeros((R, D), jnp.float32) if xs_init is None else xs_init)

    def blk(b, be, nu):
        return jnp.minimum(b, nu[0] - 1)

    ys = pl.pallas_call(
        _expert_kernel,
        grid_spec=pltpu.PrefetchScalarGridSpec(
            num_scalar_prefetch=2,
            grid=(nb,),
            in_specs=[pl.BlockSpec((BLK, D), lambda b, be, nu: (blk(b, be, nu), 0)),
                      pl.BlockSpec((1, 1, D, 2 * F), lambda b, be, nu: (layer, be[blk(b, be, nu)], 0, 0)),
                      pl.BlockSpec((1, 1, F, D), lambda b, be, nu: (layer, be[blk(b, be, nu)], 0, 0))],
            out_specs=pl.BlockSpec((BLK, D), lambda b, be, nu: (b, 0)),
            scratch_shapes=[pltpu.VMEM((D, 2 * F), jnp.bfloat16), pltpu.VMEM((F, D), jnp.bfloat16)],
        ),
        out_shape=jax.ShapeDtypeStruct((R, D), jnp.float32),
        compiler_params=pltpu.CompilerParams(dimension_semantics=("arbitrary",), vmem_limit_bytes=VMEM_LIMIT),
        name="moe_experts",
    )(blk_e, nused, xs, w_gu, w_down)

    return xs, pl.pallas_call(
        _combine_kernel,
        grid_spec=pltpu.PrefetchScalarGridSpec(
            num_scalar_prefetch=1,
            grid=(N // TM,),
            in_specs=[pl.BlockSpec(memory_space=pl.ANY),
                      pl.BlockSpec((TM, ROUTE_W), lambda i, d: (i, 0)),
                      pl.BlockSpec((TM, D), lambda i, d: (i, 0)),
                      pl.BlockSpec((TM, p.shape[1]), lambda i, d: (i, 0)),
                      pl.BlockSpec((1, D), lambda i, d: (0, 0)),
                      pl.BlockSpec((1, D), lambda i, d: (0, 0)),
                      pl.BlockSpec((D, D), lambda i, d: (0, 0)),
                      pl.BlockSpec((p.shape[1], D), lambda i, d: (0, 0))],
            out_specs=pl.BlockSpec((TM, D), lambda i, d: (i, 0)),
            scratch_shapes=[pltpu.VMEM((2, TOPK_IN_GROUP, TM, D), jnp.float32),
                            pltpu.SemaphoreType.DMA((2,))],
        ),
        out_shape=jax.ShapeDtypeStruct((N, D), jnp.float32),
        compiler_params=pltpu.CompilerParams(dimension_semantics=("arbitrary",), vmem_limit_bytes=VMEM_LIMIT),
        name="moe_combine",
    )(dest, ys, route, xt, p, ln_g[None, :], ln_b[None, :],
      w_gate.astype(jnp.bfloat16), w_proj.astype(jnp.bfloat16))


def kernel(x, p, ab_w_in, ret_gn_g, pool_w, pool_scale, ab_w_out, c_w_in, c_q_norm, c_kv_norm, c_w_uq, c_w_uk, c_w_uv, c_w_out, ln1_g, ln1_b, ln2_g, ln2_b, moe_w_group, moe_b_group, moe_w_expert, moe_b_expert, moe_w_gu, moe_w_down, ple_w_gate, ple_w_proj):
    B, T, D = x.shape
    N = B * T
    xs_buf = None
    for i in range(DEPTH):
        j = i // 2
        wr, br = _router_weights(moe_w_group[i], moe_b_group[i], moe_w_expert[i], moe_b_expert[i])
        if i % 2 == 0:
            x1, route, cnt, routeT, xs_buf = _mixer_ab_layer(x, ab_w_in[j], ret_gn_g[j], pool_w[j], pool_scale[j],
                                                             ab_w_out[j], ln1_g[i], ln1_b[i], wr, br)
            x1, route = x1.reshape(N, D), route.reshape(N, ROUTE_W)
        else:
            o = _mixer_c(x, c_w_in[j], c_q_norm[j], c_kv_norm[j], c_w_uq[j], c_w_uk[j], c_w_uv[j])
            x1, route, cnt, routeT = _outproj_layer(o.reshape(N, -1), x.reshape(N, D), c_w_out[j],
                                                    ln1_g[i], ln1_b[i], wr, br)
        xs_buf, x = _moe_layer(i, x1, route, cnt, routeT, p[i].reshape(N, -1), moe_w_gu, moe_w_down,
                               ln2_g[i], ln2_b[i], ple_w_gate[i], ple_w_proj[i], xs_buf)
        x = x.reshape(B, T, D)
    return x
```
